```python
import jax, jax.numpy as jnp
from jax import lax
import numpy as np

D_MODEL = 2048
BATCH = 1
SEQ = 16384
DEPTH = 1

LRU_WIDTH = D_MODEL
LRU_BLOCKS = 16
LRU_BLOCK_DIM = LRU_WIDTH // LRU_BLOCKS
LRU_C = 8.0
CONV_WIDTH = 4
CONV_LEFT = 2
N_HEADS = 16
N_KV_HEADS = 4
HEAD_DIM = 128
GROUP = N_HEADS // N_KV_HEADS
ATTN_WIDTH = N_HEADS * HEAD_DIM
KV_WIDTH = N_KV_HEADS * HEAD_DIM
WINDOW = 128
BLOCK = 128
ROPE_THETA = 500000.0
ROT_DIM = HEAD_DIM // 4
NORM_EPS = 1e-6
IN_SPLITS = (LRU_WIDTH, LRU_WIDTH, ATTN_WIDTH, KV_WIDTH, KV_WIDTH, ATTN_WIDTH, D_MODEL, D_MODEL)
IN_WIDTH = sum(IN_SPLITS)

kernel_name = 'hybrid_rglru_swa_gqa_encoder_block'


def rms_norm(x, w):
    xf = x.astype(jnp.float32)
    y = xf * lax.rsqrt(jnp.mean(xf * xf, axis=-1, keepdims=True) + NORM_EPS)
    return (y * w.astype(jnp.float32)).astype(x.dtype)


def centred_depthwise_conv(u, w, b):
    s = u.shape[1]
    up = jnp.pad(u, ((0, 0), (CONV_LEFT, CONV_WIDTH - 1 - CONV_LEFT), (0, 0)))
    y = b
    for tap in range(CONV_WIDTH):
        y = y + up[:, tap:tap + s] * w[tap]
    return y


def _linear_combine(e1, e2):
    a1, b1 = e1
    a2, b2 = e2
    return a1 * a2, a2 * b1 + b2


def rg_lru(u, w_r, b_r, w_i, b_i, lam, reverse):
    bsz, s, _ = u.shape
    ub = u.reshape(bsz, s, LRU_BLOCKS, LRU_BLOCK_DIM)
    gate_r = jax.nn.sigmoid(jnp.einsum('bsni,nij->bsnj', ub, w_r).reshape(bsz, s, LRU_WIDTH) + b_r)
    gate_i = jax.nn.sigmoid(jnp.einsum('bsni,nij->bsnj', ub, w_i).reshape(bsz, s, LRU_WIDTH) + b_i)
    log_a = -LRU_C * gate_r.astype(jnp.float32) * jax.nn.softplus(-lam.astype(jnp.float32))
    a = jnp.exp(log_a)
    in_scale = jnp.sqrt(-jnp.expm1(2.0 * log_a))
    b = in_scale * (gate_i * u).astype(jnp.float32)
    _, h = lax.associative_scan(_linear_combine, (a, b), axis=1, reverse=reverse)
    return h


def partial_rope(t, cos, sin):
    half = ROT_DIM // 2
    tf = t[..., :ROT_DIM].astype(jnp.float32)
    t1, t2 = tf[..., :half], tf[..., half:]
    c = cos[None, :, None, :]
    sn = sin[None, :, None, :]
    rot = jnp.concatenate([t1 * c - t2 * sn, t2 * c + t1 * sn], axis=-1).astype(t.dtype)
    return jnp.concatenate([rot, t[..., ROT_DIM:]], axis=-1)


def windowed_gqa(q, k, v, sink):
    bsz, s = q.shape[:2]
    nb = s // BLOCK
    qb = q.reshape(bsz, nb, BLOCK, N_KV_HEADS, GROUP, HEAD_DIM)

    def band(t):
        tp = jnp.pad(t, ((0, 0), (BLOCK, BLOCK), (0, 0), (0, 0)))
        tp = tp.reshape(bsz, nb + 2, BLOCK, N_KV_HEADS, HEAD_DIM)
        return jnp.concatenate([tp[:, :-2], tp[:, 1:-1], tp[:, 2:]], axis=2)

    kw, vw = band(k), band(v)
    scores = jnp.einsum('bnqkgd,bnskd->bnkgqs', qb, kw).astype(jnp.float32) * (HEAD_DIM ** -0.5)
    q_idx = jnp.arange(BLOCK)[:, None]
    s_idx = jnp.arange(3 * BLOCK)[None, :]
    band_ok = jnp.abs(s_idx - BLOCK - q_idx) <= WINDOW
    key_pos = (jnp.arange(nb)[:, None] - 1) * BLOCK + jnp.arange(3 * BLOCK)[None, :]
    in_range = (key_pos >= 0) & (key_pos < s)
    mask = band_ok[None] & in_range[:, None, :]
    scores = jnp.where(mask[None, :, None, None], scores, -1e30)
    sink_l = sink.astype(jnp.float32).reshape(1, 1, N_KV_HEADS, GROUP, 1, 1)
    m = jnp.maximum(scores.max(axis=-1, keepdims=True), sink_l)
    e = jnp.exp(scores - m)
    p = e / (e.sum(axis=-1, keepdims=True) + jnp.exp(sink_l - m))
    out = jnp.einsum('bnkgqs,bnskd->bnqkgd', p.astype(v.dtype), vw)
    return out.reshape(bsz, s, ATTN_WIDTH)


def setup_inputs(seed: int = 0) -> dict:
    key = jax.random.key(seed)
    ks = jax.random.split(key, 16)

    def nrm(k, shape, fan_in):
        return jax.random.normal(k, shape, jnp.float32) * (fan_in ** -0.5)

    L = DEPTH
    x = jax.random.normal(ks[0], (BATCH, SEQ, D_MODEL), jnp.float32)
    norm_pre_w = 1.0 + 0.02 * jax.random.normal(ks[1], (L, D_MODEL), jnp.float32)
    w_in = nrm(ks[2], (L, D_MODEL, IN_WIDTH), D_MODEL)
    conv_w = nrm(ks[3], (L, CONV_WIDTH, LRU_WIDTH), CONV_WIDTH)
    conv_b = 0.01 * jax.random.normal(ks[4], (L, LRU_WIDTH), jnp.float32)
    lru_w_r = nrm(ks[5], (L, 2, LRU_BLOCKS, LRU_BLOCK_DIM, LRU_BLOCK_DIM), LRU_BLOCK_DIM)
    lru_b_r = 0.01 * jax.random.normal(ks[6], (L, 2, LRU_WIDTH), jnp.float32)
    lru_w_i = nrm(ks[7], (L, 2, LRU_BLOCKS, LRU_BLOCK_DIM, LRU_BLOCK_DIM), LRU_BLOCK_DIM)
    lru_b_i = 0.01 * jax.random.normal(ks[8], (L, 2, LRU_WIDTH), jnp.float32)
    a_c = jax.random.uniform(ks[9], (L, 2, LRU_WIDTH), jnp.float32, minval=0.9, maxval=0.999)
    sig = a_c ** (1.0 / LRU_C)
    lru_lambda = jnp.log(sig) - jnp.log1p(-sig)
    attn_sink = jax.random.normal(ks[10], (L, N_HEADS), jnp.float32)
    w_proj_a = nrm(ks[11], (L, LRU_WIDTH, D_MODEL), LRU_WIDTH)
    w_proj_b = nrm(ks[12], (L, ATTN_WIDTH, D_MODEL), ATTN_WIDTH)
    w_out = nrm(ks[13], (L, D_MODEL, D_MODEL), D_MODEL)
    norm_post_w = 1.0 + 0.02 * jax.random.normal(ks[14], (L, D_MODEL), jnp.float32)
    return {'x': x, 'norm_pre_w': norm_pre_w, 'w_in': w_in, 'conv_w': conv_w, 'conv_b': conv_b,
            'lru_w_r': lru_w_r, 'lru_b_r': lru_b_r, 'lru_w_i': lru_w_i, 'lru_b_i': lru_b_i,
            'lru_lambda': lru_lambda, 'attn_sink': attn_sink, 'w_proj_a': w_proj_a,
            'w_proj_b': w_proj_b, 'w_out': w_out, 'norm_post_w': norm_post_w}


def reference(x, norm_pre_w, w_in, conv_w, conv_b, lru_w_r, lru_b_r, lru_w_i, lru_b_i,
              lru_lambda, attn_sink, w_proj_a, w_proj_b, w_out, norm_post_w):
    bsz, s, _ = x.shape
    pos = jnp.arange(s, dtype=jnp.float32)
    inv_freq = ROPE_THETA ** (-jnp.arange(0, ROT_DIM, 2, dtype=jnp.float32) / ROT_DIM)
    ang = pos[:, None] * inv_freq[None, :]
    cos, sin = jnp.cos(ang), jnp.sin(ang)
    split_at = np.cumsum(IN_SPLITS)[:-1].tolist()
    for l in range(DEPTH):
        xn = rms_norm(x, norm_pre_w[l])
        z = xn @ w_in[l]
        u_lru, g_lru, q, k, v, g_attn, m_lru, m_attn = jnp.split(z, split_at, axis=-1)
        u = centred_depthwise_conv(u_lru, conv_w[l], conv_b[l])
        h = (rg_lru(u, lru_w_r[l, 0], lru_b_r[l, 0], lru_w_i[l, 0], lru_b_i[l, 0], lru_lambda[l, 0], False)
             + rg_lru(u, lru_w_r[l, 1], lru_b_r[l, 1], lru_w_i[l, 1], lru_b_i[l, 1], lru_lambda[l, 1], True))
        y_a = h.astype(x.dtype) * jax.nn.silu(g_lru)
        q = partial_rope(q.reshape(bsz, s, N_HEADS, HEAD_DIM), cos, sin)
        k = partial_rope(k.reshape(bsz, s, N_KV_HEADS, HEAD_DIM), cos, sin)
        v = v.reshape(bsz, s, N_KV_HEADS, HEAD_DIM)
        y_b = windowed_gqa(q, k, v, attn_sink[l]) * jax.nn.silu(g_attn)
        merged = (jax.nn.sigmoid(m_lru) * (y_a @ w_proj_a[l])
                  + jax.nn.sigmoid(m_attn) * (y_b @ w_proj_b[l]))
        x = x + rms_norm(merged @ w_out[l], norm_post_w[l])
    return x
```

```python
import functools

import jax
import jax.numpy as jnp
from jax import lax
from jax.experimental import pallas as pl
from jax.experimental.pallas import tpu as pltpu

F32 = jnp.float32
BF16 = jnp.bfloat16

D_MODEL = 2048
LRU_BLOCK_DIM = 128
LRU_C = 8.0
N_HEADS = 16
N_KV_HEADS = 4
GROUP = N_HEADS // N_KV_HEADS
HEAD_DIM = 128
BLOCK = 128
WINDOW = 128
ROPE_THETA = 500000.0
ROT_DIM = HEAD_DIM // 4
NORM_EPS = 1e-6
MASK_VALUE = -1e30

COL_U, COL_G, COL_Q, COL_K, COL_V, COL_GA, COL_MA, COL_MB = (
    0, 2048, 4096, 6144, 6656, 7168, 9216, 11264)
IN_WIDTH = 13312

SUBLANES = 8
VMEM_LIMIT = 56 * 1024 * 1024


def _sigmoid(x):
    return 0.5 * jnp.tanh(0.5 * x) + 0.5


INPROJ_TM = 1024
INPROJ_TN = 1024
NORM_ROWS = 64


def _inproj_kernel(x_ref, nw_ref, w_ref, o_ref, xn_ref):
    @pl.when(pl.program_id(1) == 0)
    def _():
        def body(r, carry):
            rows = pl.ds(pl.multiple_of(r * NORM_ROWS, NORM_ROWS), NORM_ROWS)
            xv = x_ref[rows, :]
            ms = jnp.mean(xv * xv, axis=-1, keepdims=True)
            xn_ref[rows, :] = (xv * lax.rsqrt(ms + NORM_EPS) * nw_ref[...]).astype(BF16)
            return carry
        lax.fori_loop(0, INPROJ_TM // NORM_ROWS, body, 0)

    o_ref[...] = jnp.dot(xn_ref[...], w_ref[...], preferred_element_type=F32)


def _inproj(x2, norm_w, w_in_bf16):
    s = x2.shape[0]
    return pl.pallas_call(
        _inproj_kernel,
        out_shape=jax.ShapeDtypeStruct((s, IN_WIDTH), F32),
        grid=(s // INPROJ_TM, IN_WIDTH // INPROJ_TN),
        in_specs=[
            pl.BlockSpec((INPROJ_TM, D_MODEL), lambda i, j: (i, 0)),
            pl.BlockSpec((1, D_MODEL), lambda i, j: (0, 0)),
            pl.BlockSpec((D_MODEL, INPROJ_TN), lambda i, j: (0, j)),
        ],
        out_specs=pl.BlockSpec((INPROJ_TM, INPROJ_TN), lambda i, j: (i, j)),
        scratch_shapes=[pltpu.VMEM((INPROJ_TM, D_MODEL), BF16)],
        compiler_params=pltpu.CompilerParams(
            dimension_semantics=("parallel", "arbitrary"),
            vmem_limit_bytes=VMEM_LIMIT),
        name="inproj",
    )(x2, norm_w, w_in_bf16)


LRU_T = 512
LRU_CW = 512
LRU_CONV_ROWS = 64
LRU_TILE = 32


def _lru_kernel(reverse, final, n_chunks, *refs):
    if final:
        (up_ref, uc_ref, un_ref, cw_ref, cb_ref, wr_ref, wi_ref, br_ref, bi_ref, lam_ref,
         g_ref, hf_ref, o_ref, ubuf, uconv, zr_s, zi_s, a_s, b_s, h_s, carry_s) = refs
    else:
        (up_ref, uc_ref, un_ref, cw_ref, cb_ref, wr_ref, wi_ref, br_ref, bi_ref, lam_ref,
         o_ref, ubuf, uconv, zr_s, zi_s, a_s, b_s, carry_s) = refs
        h_s = o_ref
    t_rows, cw = LRU_T, LRU_CW
    i = pl.program_id(1)
    c = (n_chunks - 1 - i) if reverse else i

    @pl.when(i == 0)
    def _():
        carry_s[...] = jnp.zeros_like(carry_s)

    ubuf[0:SUBLANES, :] = jnp.where(c > 0, up_ref[...], 0.0)
    ubuf[SUBLANES:SUBLANES + t_rows, :] = uc_ref[...]
    ubuf[SUBLANES + t_rows:, :] = jnp.where(c < n_chunks - 1, un_ref[...], 0.0)

    for r0 in range(0, t_rows, LRU_CONV_ROWS):
        y = cb_ref[...]
        for tap in range(4):
            lo = SUBLANES + r0 + tap - 2
            y = y + ubuf[lo:lo + LRU_CONV_ROWS, :] * cw_ref[tap:tap + 1, :]
        uconv[r0:r0 + LRU_CONV_ROWS, :] = y

    for k in range(cw // LRU_BLOCK_DIM):
        cols = slice(k * LRU_BLOCK_DIM, (k + 1) * LRU_BLOCK_DIM)
        ub = uconv[:, cols].astype(BF16)
        zr_s[:, cols] = jnp.dot(ub, wr_ref[k], preferred_element_type=F32)
        zi_s[:, cols] = jnp.dot(ub, wi_ref[k], preferred_element_type=F32)

    nl = -lam_ref[...]
    softplus = jnp.maximum(nl, 0.0) + jnp.log1p(jnp.exp(-jnp.abs(nl)))
    decay = -LRU_C * softplus
    br = br_ref[...]
    bi = bi_ref[...]

    groups = LRU_TILE // SUBLANES
    row_in_group = lax.broadcasted_iota(jnp.int32, (groups, SUBLANES, cw), 1)

    def tile_body(t, carry):
        rows = pl.ds(pl.multiple_of(t * LRU_TILE, LRU_TILE), LRU_TILE)
        r = _sigmoid(zr_s[rows, :] + br)
        gi = _sigmoid(zi_s[rows, :] + bi)
        log_a = r * decay
        th = jnp.tanh(0.5 * log_a)
        inv = 1.0 / (1.0 - th)
        a = (1.0 + th) * inv
        in_scale = 2.0 * jnp.sqrt(-th) * inv
        b = in_scale * (gi * uconv[rows, :])
        a3 = a.reshape(groups, SUBLANES, cw)
        b3 = b.reshape(groups, SUBLANES, cw)
        for d in (1, 2, 4):
            if reverse:
                shift, valid = SUBLANES - d, row_in_group < SUBLANES - d
            else:
                shift, valid = d, row_in_group >= d
            a_sh = pltpu.roll(a3, shift, axis=1)
            b_sh = pltpu.roll(b3, shift, axis=1)
            b3 = b3 + jnp.where(valid, a3 * b_sh, 0.0)
            a3 = a3 * jnp.where(valid, a_sh, 1.0)
        a_s[rows, :] = a3.reshape(LRU_TILE, cw)
        b_s[rows, :] = b3.reshape(LRU_TILE, cw)
        return carry

    lax.fori_loop(0, t_rows // LRU_TILE, tile_body, 0)

    n_groups = t_rows // SUBLANES

    def carry_body(j, carry):
        jj = (n_groups - 1 - j) if reverse else j
        rows = pl.ds(pl.multiple_of(jj * SUBLANES, SUBLANES), SUBLANES)
        h = b_s[rows, :] + a_s[rows, :] * carry
        h_s[rows, :] = h
        return h[0:1, :] if reverse else h[SUBLANES - 1:SUBLANES, :]

    carry_s[0:1, :] = lax.fori_loop(0, n_groups, carry_body, carry_s[0:1, :], unroll=8)

    if final:
        def out_body(t, carry):
            rows = pl.ds(pl.multiple_of(t * LRU_TILE, LRU_TILE), LRU_TILE)
            g = g_ref[rows, :]
            h = h_s[rows, :] + hf_ref[rows, :]
            o_ref[rows, :] = (h * (g * _sigmoid(g))).astype(o_ref.dtype)
            return carry
        lax.fori_loop(0, t_rows // LRU_TILE, out_body, 0)


def _lru_call(z, conv_w, conv_b, w_r, w_i, b_r, b_i, lam, *, reverse, h_fwd=None):
    s = z.shape[0]
    n_chunks = s // LRU_T
    n_cb = D_MODEL // LRU_CW
    final = h_fwd is not None
    halo_blocks = LRU_T // SUBLANES
    n_halo = s // SUBLANES

    def chunk(i):
        return (n_chunks - 1 - i) if reverse else i

    in_specs = [
        pl.BlockSpec((SUBLANES, LRU_CW),
                     lambda n, i: (jnp.maximum(chunk(i) * halo_blocks - 1, 0), n)),
        pl.BlockSpec((LRU_T, LRU_CW), lambda n, i: (chunk(i), n)),
        pl.BlockSpec((SUBLANES, LRU_CW),
                     lambda n, i: (jnp.minimum((chunk(i) + 1) * halo_blocks, n_halo - 1), n)),
        pl.BlockSpec((4, LRU_CW), lambda n, i: (0, n)),
        pl.BlockSpec((1, LRU_CW), lambda n, i: (0, n)),
        pl.BlockSpec((LRU_CW // LRU_BLOCK_DIM, LRU_BLOCK_DIM, LRU_BLOCK_DIM), lambda n, i: (n, 0, 0)),
        pl.BlockSpec((LRU_CW // LRU_BLOCK_DIM, LRU_BLOCK_DIM, LRU_BLOCK_DIM), lambda n, i: (n, 0, 0)),
        pl.BlockSpec((1, LRU_CW), lambda n, i: (0, n)),
        pl.BlockSpec((1, LRU_CW), lambda n, i: (0, n)),
        pl.BlockSpec((1, LRU_CW), lambda n, i: (0, n)),
    ]
    args = [z, z, z, conv_w, conv_b, w_r, w_i, b_r, b_i, lam]
    scratch = [
        pltpu.VMEM((LRU_T + 2 * SUBLANES, LRU_CW), F32),
        pltpu.VMEM((LRU_T, LRU_CW), F32),
        pltpu.VMEM((LRU_T, LRU_CW), F32),
        pltpu.VMEM((LRU_T, LRU_CW), F32),
        pltpu.VMEM((LRU_T, LRU_CW), F32),
        pltpu.VMEM((LRU_T, LRU_CW), F32),
    ]
    if final:
        g_col0 = COL_G // LRU_CW
        in_specs += [
            pl.BlockSpec((LRU_T, LRU_CW), lambda n, i: (chunk(i), g_col0 + n)),
            pl.BlockSpec((LRU_T, LRU_CW), lambda n, i: (chunk(i), n)),
        ]
        args += [z, h_fwd]
        scratch.append(pltpu.VMEM((LRU_T, LRU_CW), F32))
        out_dtype = BF16
    else:
        out_dtype = F32
    scratch.append(pltpu.VMEM((SUBLANES, LRU_CW), F32))

    return pl.pallas_call(
        functools.partial(_lru_kernel, reverse, final, n_chunks),
        out_shape=jax.ShapeDtypeStruct((s, D_MODEL), out_dtype),
        grid=(n_cb, n_chunks),
        in_specs=in_specs,
        out_specs=pl.BlockSpec((LRU_T, LRU_CW), lambda n, i: (chunk(i), n)),
        scratch_shapes=scratch,
        compiler_params=pltpu.CompilerParams(
            dimension_semantics=("parallel", "arbitrary"),
            vmem_limit_bytes=VMEM_LIMIT),
        name="lru_bwd" if reverse else "lru_fwd",
    )(*args)


def _rope(t, cos_t, sin_t, lane):
    half = ROT_DIM // 2
    partner = jnp.where(lane < half,
                        pltpu.roll(t, HEAD_DIM - half, axis=1),
                        pltpu.roll(t, half, axis=1))
    return t * cos_t + partner * sin_t


def _attn_kernel(n_blocks, sink_ref, q_ref, kl_ref, kc_ref, kr_ref, vl_ref, vc_ref, vr_ref,
                 g_ref, cq_ref, sq_ref, cl_ref, sl_ref, cr_ref, sr_ref, o_ref):
    kh = pl.program_id(0)
    nb = pl.program_id(1)
    lane = lax.broadcasted_iota(jnp.int32, (BLOCK, HEAD_DIM), 1)
    scale = HEAD_DIM ** -0.5

    cq, sq = cq_ref[...], sq_ref[...]
    q_all = q_ref[...]
    q_rows = [(_rope(q_all[:, g * HEAD_DIM:(g + 1) * HEAD_DIM], cq, sq, lane)).astype(BF16)
              for g in range(GROUP)]
    q = jnp.concatenate(q_rows, axis=0)
    k = jnp.concatenate([
        _rope(kl_ref[...], cl_ref[...], sl_ref[...], lane).astype(BF16),
        _rope(kc_ref[...], cq, sq, lane).astype(BF16),
        _rope(kr_ref[...], cr_ref[...], sr_ref[...], lane).astype(BF16)], axis=0)
    v = jnp.concatenate([vl_ref[...].astype(BF16), vc_ref[...].astype(BF16),
                         vr_ref[...].astype(BF16)], axis=0)

    scores = lax.dot_general(q, k, (((1,), (1,)), ((), ())), preferred_element_type=F32) * scale

    rows = GROUP * BLOCK
    q_idx = lax.broadcasted_iota(jnp.int32, (BLOCK, 3 * BLOCK), 0)
    s_idx = lax.broadcasted_iota(jnp.int32, (BLOCK, 3 * BLOCK), 1)
    rel = s_idx - BLOCK - q_idx
    ok = (rel <= WINDOW) & (rel >= -WINDOW)
    ok = ok & ((s_idx >= BLOCK) | (nb > 0)) & ((s_idx < 2 * BLOCK) | (nb < n_blocks - 1))
    bias = jnp.where(ok, 0.0, MASK_VALUE)
    bias = jnp.concatenate([bias] * GROUP, axis=0)
    scores = jnp.where(bias == 0.0, scores, MASK_VALUE)

    head_of_row = lax.broadcasted_iota(jnp.int32, (rows, 1), 0) >> 7
    sink = jnp.zeros((rows, 1), F32)
    for g in range(GROUP):
        sink = jnp.where(head_of_row == g, sink_ref[kh * GROUP + g], sink)

    m = jnp.maximum(jnp.max(scores, axis=-1, keepdims=True), sink)
    e = jnp.exp(scores - m)
    denom = jnp.sum(e, axis=-1, keepdims=True) + jnp.exp(sink - m)
    out = jnp.dot(e.astype(BF16), v, preferred_element_type=F32) / denom

    gate = g_ref[...]
    for g in range(GROUP):
        cols = slice(g * HEAD_DIM, (g + 1) * HEAD_DIM)
        gg = gate[:, cols]
        o_ref[:, cols] = (out[g * BLOCK:(g + 1) * BLOCK, :] * (gg * _sigmoid(gg))).astype(o_ref.dtype)


def _attention(z, sink, cos_t, sin_t):
    s = z.shape[0]
    n_blocks = s // BLOCK
    qw = GROUP * HEAD_DIM

    def left(nb):
        return jnp.maximum(nb - 1, 0)

    def right(nb):
        return jnp.minimum(nb + 1, n_blocks - 1)

    kcol = COL_K // HEAD_DIM
    vcol = COL_V // HEAD_DIM
    in_specs = [
        pl.BlockSpec(memory_space=pltpu.SMEM),
        pl.BlockSpec((BLOCK, qw), lambda kh, nb: (nb, COL_Q // qw + kh)),
        pl.BlockSpec((BLOCK, HEAD_DIM), lambda kh, nb: (left(nb), kcol + kh)),
        pl.BlockSpec((BLOCK, HEAD_DIM), lambda kh, nb: (nb, kcol + kh)),
        pl.BlockSpec((BLOCK, HEAD_DIM), lambda kh, nb: (right(nb), kcol + kh)),
        pl.BlockSpec((BLOCK, HEAD_DIM), lambda kh, nb: (left(nb), vcol + kh)),
        pl.BlockSpec((BLOCK, HEAD_DIM), lambda kh, nb: (nb, vcol + kh)),
        pl.BlockSpec((BLOCK, HEAD_DIM), lambda kh, nb: (right(nb), vcol + kh)),
        pl.BlockSpec((BLOCK, qw), lambda kh, nb: (nb, COL_GA // qw + kh)),
        pl.BlockSpec((BLOCK, HEAD_DIM), lambda kh, nb: (nb, 0)),
        pl.BlockSpec((BLOCK, HEAD_DIM), lambda kh, nb: (nb, 0)),
        pl.BlockSpec((BLOCK, HEAD_DIM), lambda kh, nb: (left(nb), 0)),
        pl.BlockSpec((BLOCK, HEAD_DIM), lambda kh, nb: (left(nb), 0)),
        pl.BlockSpec((BLOCK, HEAD_DIM), lambda kh, nb: (right(nb), 0)),
        pl.BlockSpec((BLOCK, HEAD_DIM), lambda kh, nb: (right(nb), 0)),
    ]
    return pl.pallas_call(
        functools.partial(_attn_kernel, n_blocks),
        out_shape=jax.ShapeDtypeStruct((s, N_HEADS * HEAD_DIM), BF16),
        grid=(N_KV_HEADS, n_blocks),
        in_specs=in_specs,
        out_specs=pl.BlockSpec((BLOCK, qw), lambda kh, nb: (nb, kh)),
        compiler_params=pltpu.CompilerParams(
            dimension_semantics=("parallel", "parallel"),
            vmem_limit_bytes=VMEM_LIMIT),
        name="attn",
    )(sink, z, z, z, z, z, z, z, z, cos_t, sin_t, cos_t, sin_t, cos_t, sin_t)


MERGE_TM = 256
MERGE_HALF = D_MODEL // 2


def _merge_kernel(ya_ref, yb_ref, ma0_ref, ma1_ref, mb0_ref, mb1_ref, x_ref,
                  wa_ref, wb_ref, wo_ref, nw_ref, o_ref):
    pa = jnp.dot(ya_ref[...], wa_ref[...], preferred_element_type=F32)
    pb = jnp.dot(yb_ref[...], wb_ref[...], preferred_element_type=F32)
    half = MERGE_HALF
    merged = jnp.concatenate([
        _sigmoid(ma0_ref[...]) * pa[:, :half] + _sigmoid(mb0_ref[...]) * pb[:, :half],
        _sigmoid(ma1_ref[...]) * pa[:, half:] + _sigmoid(mb1_ref[...]) * pb[:, half:]], axis=1)
    y = jnp.dot(merged.astype(BF16), wo_ref[...], preferred_element_type=F32)
    ms = jnp.mean(y * y, axis=-1, keepdims=True)
    o_ref[...] = x_ref[...] + y * lax.rsqrt(ms + NORM_EPS) * nw_ref[...]


def _merge(y_a, y_b, z, x2, wa, wb, wo, norm_w):
    s = x2.shape[0]
    row = lambda i: (i, 0)
    const = lambda i: (0, 0)
    resident = functools.partial(pl.BlockSpec, pipeline_mode=pl.Buffered(1))
    return pl.pallas_call(
        _merge_kernel,
        out_shape=jax.ShapeDtypeStruct((s, D_MODEL), F32),
        grid=(s // MERGE_TM,),
        in_specs=[
            pl.BlockSpec((MERGE_TM, D_MODEL), row),
            pl.BlockSpec((MERGE_TM, D_MODEL), row),
            pl.BlockSpec((MERGE_TM, MERGE_HALF), lambda i: (i, COL_MA // MERGE_HALF)),
            pl.BlockSpec((MERGE_TM, MERGE_HALF), lambda i: (i, COL_MA // MERGE_HALF + 1)),
            pl.BlockSpec((MERGE_TM, MERGE_HALF), lambda i: (i, COL_MB // MERGE_HALF)),
            pl.BlockSpec((MERGE_TM, MERGE_HALF), lambda i: (i, COL_MB // MERGE_HALF + 1)),
            pl.BlockSpec((MERGE_TM, D_MODEL), row),
            resident((D_MODEL, D_MODEL), const),
            resident((D_MODEL, D_MODEL), const),
            resident((D_MODEL, D_MODEL), const),
            pl.BlockSpec((1, D_MODEL), const),
        ],
        out_specs=pl.BlockSpec((MERGE_TM, D_MODEL), row),
        compiler_params=pltpu.CompilerParams(
            dimension_semantics=("parallel",),
            vmem_limit_bytes=VMEM_LIMIT),
        name="merge",
    )(y_a, y_b, z, z, z, z, x2, wa, wb, wo, norm_w)


def _rope_tables(s):
    pos = jnp.arange(s, dtype=F32)
    inv_freq = ROPE_THETA ** (-jnp.arange(0, ROT_DIM, 2, dtype=F32) / ROT_DIM)
    ang = pos[:, None] * inv_freq[None, :]
    cos, sin = jnp.cos(ang), jnp.sin(ang)
    pad = HEAD_DIM - ROT_DIM
    cos_t = jnp.concatenate([cos, cos, jnp.ones((s, pad), F32)], axis=-1)
    sin_t = jnp.concatenate([-sin, sin, jnp.zeros((s, pad), F32)], axis=-1)
    return cos_t, sin_t


def kernel(x, norm_pre_w, w_in, conv_w, conv_b, lru_w_r, lru_b_r, lru_w_i, lru_b_i,
           lru_lambda, attn_sink, w_proj_a, w_proj_b, w_out, norm_post_w):
    bsz, s, d = x.shape
    depth = w_in.shape[0]
    cos_t, sin_t = _rope_tables(s)
    outs = []
    for bi in range(bsz):
        xb = x[bi]
        for l in range(depth):
            z = _inproj(xb, norm_pre_w[l][None, :], w_in[l].astype(BF16))
            lru_args = (z, conv_w[l], conv_b[l][None, :])
            h_fwd = _lru_call(*lru_args, lru_w_r[l, 0].astype(BF16), lru_w_i[l, 0].astype(BF16),
                              lru_b_r[l, 0][None, :], lru_b_i[l, 0][None, :],
                              lru_lambda[l, 0][None, :], reverse=False)
            y_a = _lru_call(*lru_args, lru_w_r[l, 1].astype(BF16), lru_w_i[l, 1].astype(BF16),
                            lru_b_r[l, 1][None, :], lru_b_i[l, 1][None, :],
                            lru_lambda[l, 1][None, :], reverse=True, h_fwd=h_fwd)
            y_b = _attention(z, attn_sink[l], cos_t, sin_t)
            xb = _merge(y_a, y_b, z, xb, w_proj_a[l].astype(BF16), w_proj_b[l].astype(BF16),
                        w_out[l].astype(BF16), norm_post_w[l][None, :])
        outs.append(xb)
    return jnp.stack(outs, axis=0)
```

```python
import functools
import math

import jax
import jax.numpy as jnp
from jax import lax
from jax.experimental import pallas as pl
from jax.experimental.pallas import tpu as pltpu

F32 = jnp.float32
BF16 = jnp.bfloat16

D_MODEL = 2048
LRU_BLOCK_DIM = 128
LRU_C = 8.0
N_HEADS = 16
N_KV_HEADS = 4
GROUP = N_HEADS // N_KV_HEADS
HEAD_DIM = 128
KV_WIDTH = N_KV_HEADS * HEAD_DIM
BLOCK = 128
WINDOW = 128
ROPE_THETA = 500000.0
ROT_DIM = HEAD_DIM // 4
NORM_EPS = 1e-6
MASK_VALUE = -1e30
LOG2E = math.log2(math.e)

COL_U, COL_G, COL_Q, COL_K, COL_V, COL_GA, COL_MA, COL_MB = (
    0, 2048, 4096, 6144, 6656, 7168, 9216, 11264)
IN_WIDTH = 13312

B_SG_LRU, B_SG_ATT, B_GM_LRU, B_GM_ATT, B_Q, B_K, B_V = (
    0, 2048, 4096, 6144, 8192, 10240, 10752)
B_WIDTH = 11264

SUBLANES = 8
LANES = 128
VMEM_LIMIT = 56 * 1024 * 1024


def _sigmoid(x):
    return 0.5 * jnp.tanh(0.5 * x) + 0.5


def _silu(x):
    hx = 0.5 * x
    return hx * jnp.tanh(hx) + hx


INPROJ_TM = 1024
INPROJ_TN = 1024
NORM_ROWS = 64
J_U, J_SILU, J_SIG, J_Q, J_KV, J_END = 0, 2, 6, 10, 12, 13


def _rope_tables_from_packed(packed, scale):
    lane = lax.broadcasted_iota(jnp.int32, packed.shape, 1)
    in_rot = lane < ROT_DIM
    cos_t = jnp.where(in_rot, packed, 1.0) * scale
    sin_t = jnp.where(in_rot, pltpu.roll(packed, LANES - ROT_DIM, axis=1), 0.0) * scale
    return cos_t, sin_t


def _rope(t, cos_t, sin_t):
    half = ROT_DIM // 2
    lane = lax.broadcasted_iota(jnp.int32, t.shape, 1)
    partner = jnp.where(lane < half,
                        pltpu.roll(t, HEAD_DIM - half, axis=1),
                        pltpu.roll(t, half, axis=1))
    return t * cos_t + partner * sin_t


def _inproj_kernel(x_ref, nw_ref, w_ref, rope_ref, u_ref, b_ref, xn_ref):
    j = pl.program_id(1)

    @pl.when(j == 0)
    def _():
        def body(r, carry):
            rows = pl.ds(pl.multiple_of(r * NORM_ROWS, NORM_ROWS), NORM_ROWS)
            xv = x_ref[rows, :]
            ms = jnp.mean(xv * xv, axis=-1, keepdims=True)
            xn_ref[rows, :] = (xv * lax.rsqrt(ms + NORM_EPS) * nw_ref[...]).astype(BF16)
            return carry
        lax.fori_loop(0, INPROJ_TM // NORM_ROWS, body, 0)

    def project():
        return jnp.dot(xn_ref[...], w_ref[...], preferred_element_type=F32)

    @pl.when(j < J_SILU)
    def _():
        u_ref[...] = project()

    @pl.when((j >= J_SILU) & (j < J_SIG))
    def _():
        b_ref[...] = _silu(project()).astype(BF16)

    @pl.when((j >= J_SIG) & (j < J_Q))
    def _():
        b_ref[...] = _sigmoid(project()).astype(BF16)

    @pl.when((j >= J_Q) & (j < J_KV))
    def _():
        cos_t, sin_t = _rope_tables_from_packed(rope_ref[...], HEAD_DIM ** -0.5 * LOG2E)
        acc = project()
        for hb in range(INPROJ_TN // HEAD_DIM):
            cols = slice(hb * HEAD_DIM, (hb + 1) * HEAD_DIM)
            b_ref[:, cols] = _rope(acc[:, cols], cos_t, sin_t).astype(BF16)

    @pl.when(j == J_KV)
    def _():
        cos_t, sin_t = _rope_tables_from_packed(rope_ref[...], 1.0)
        acc = project()
        for hb in range(N_KV_HEADS):
            cols = slice(hb * HEAD_DIM, (hb + 1) * HEAD_DIM)
            b_ref[:, cols] = _rope(acc[:, cols], cos_t, sin_t).astype(BF16)
        b_ref[:, KV_WIDTH:] = acc[:, KV_WIDTH:].astype(BF16)


def _inproj(x2, norm_w, w_cat, rope_packed):
    s = x2.shape[0]
    return pl.pallas_call(
        _inproj_kernel,
        out_shape=(jax.ShapeDtypeStruct((s, D_MODEL), F32),
                   jax.ShapeDtypeStruct((s, B_WIDTH), BF16)),
        grid=(s // INPROJ_TM, J_END),
        in_specs=[
            pl.BlockSpec((INPROJ_TM, D_MODEL), lambda i, j: (i, 0)),
            pl.BlockSpec((1, D_MODEL), lambda i, j: (0, 0)),
            pl.BlockSpec((D_MODEL, INPROJ_TN), lambda i, j: (0, j)),
            pl.BlockSpec((INPROJ_TM, LANES), lambda i, j: (i, 0)),
        ],
        out_specs=(
            pl.BlockSpec((INPROJ_TM, INPROJ_TN), lambda i, j: (i, jnp.minimum(j, J_SILU - 1))),
            pl.BlockSpec((INPROJ_TM, INPROJ_TN), lambda i, j: (i, jnp.maximum(j - J_SILU, 0))),
        ),
        scratch_shapes=[pltpu.VMEM((INPROJ_TM, D_MODEL), BF16)],
        compiler_params=pltpu.CompilerParams(
            dimension_semantics=("parallel", "arbitrary"),
            vmem_limit_bytes=VMEM_LIMIT),
        name="inproj",
    )(x2, norm_w, w_cat, rope_packed)


LRU_T = 512
LRU_CW = 512
LRU_CONV_ROWS = 64
LRU_TILE = 32


def _lru_kernel(reverse, final, n_chunks, *refs):
    if final:
        (up_ref, uc_ref, un_ref, cw_ref, cb_ref, wr_ref, wi_ref, br_ref, bi_ref, lam_ref,
         sg_ref, hf_ref, o_ref, ubuf, uconv, zr_s, zi_s, a_s, b_s, h_s, carry_s) = refs
    else:
        (up_ref, uc_ref, un_ref, cw_ref, cb_ref, wr_ref, wi_ref, br_ref, bi_ref, lam_ref,
         o_ref, ubuf, uconv, zr_s, zi_s, a_s, b_s, carry_s) = refs
        h_s = o_ref
    t_rows, cw = LRU_T, LRU_CW
    i = pl.program_id(1)
    c = (n_chunks - 1 - i) if reverse else i

    @pl.when(i == 0)
    def _():
        carry_s[...] = jnp.zeros_like(carry_s)

    ubuf[0:SUBLANES, :] = jnp.where(c > 0, up_ref[...], 0.0)
    ubuf[SUBLANES:SUBLANES + t_rows, :] = uc_ref[...]
    ubuf[SUBLANES + t_rows:, :] = jnp.where(c < n_chunks - 1, un_ref[...], 0.0)

    for r0 in range(0, t_rows, LRU_CONV_ROWS):
        y = cb_ref[...]
        for tap in range(4):
            lo = SUBLANES + r0 + tap - 2
            y = y + ubuf[lo:lo + LRU_CONV_ROWS, :] * cw_ref[tap:tap + 1, :]
        uconv[r0:r0 + LRU_CONV_ROWS, :] = y

    for k in range(cw // LRU_BLOCK_DIM):
        cols = slice(k * LRU_BLOCK_DIM, (k + 1) * LRU_BLOCK_DIM)
        ub = uconv[:, cols].astype(BF16)
        zr_s[:, cols] = jnp.dot(ub, wr_ref[k], preferred_element_type=F32)
        zi_s[:, cols] = jnp.dot(ub, wi_ref[k], preferred_element_type=F32)

    nl = -lam_ref[...]
    softplus = jnp.maximum(nl, 0.0) + jnp.log1p(jnp.exp(-jnp.abs(nl)))
    decay = -LRU_C * softplus
    br = br_ref[...]
    bi = bi_ref[...]

    groups = LRU_TILE // SUBLANES
    row_in_group = lax.broadcasted_iota(jnp.int32, (groups, SUBLANES, cw), 1)

    def tile_body(t, carry):
        rows = pl.ds(pl.multiple_of(t * LRU_TILE, LRU_TILE), LRU_TILE)
        r = _sigmoid(zr_s[rows, :] + br)
        gi = _sigmoid(zi_s[rows, :] + bi)
        log_a = r * decay
        th = jnp.tanh(0.5 * log_a)
        inv = 1.0 / (1.0 - th)
        a = (1.0 + th) * inv
        in_scale = 2.0 * jnp.sqrt(-th) * inv
        b = in_scale * (gi * uconv[rows, :])
        a3 = a.reshape(groups, SUBLANES, cw)
        b3 = b.reshape(groups, SUBLANES, cw)
        for d in (1, 2, 4):
            if reverse:
                shift, valid = SUBLANES - d, row_in_group < SUBLANES - d
            else:
                shift, valid = d, row_in_group >= d
            a_sh = pltpu.roll(a3, shift, axis=1)
            b_sh = pltpu.roll(b3, shift, axis=1)
            b3 = b3 + jnp.where(valid, a3 * b_sh, 0.0)
            a3 = a3 * jnp.where(valid, a_sh, 1.0)
        a_s[rows, :] = a3.reshape(LRU_TILE, cw)
        b_s[rows, :] = b3.reshape(LRU_TILE, cw)
        return carry

    lax.fori_loop(0, t_rows // LRU_TILE, tile_body, 0)

    n_groups = t_rows // SUBLANES

    def carry_body(j, carry):
        jj = (n_groups - 1 - j) if reverse else j
        rows = pl.ds(pl.multiple_of(jj * SUBLANES, SUBLANES), SUBLANES)
        h = b_s[rows, :] + a_s[rows, :] * carry
        h_s[rows, :] = h
        return h[0:1, :] if reverse else h[SUBLANES - 1:SUBLANES, :]

    carry_s[0:1, :] = lax.fori_loop(0, n_groups, carry_body, carry_s[0:1, :], unroll=8)

    if final:
        def out_body(t, carry):
            rows = pl.ds(pl.multiple_of(t * LRU_TILE, LRU_TILE), LRU_TILE)
            h = h_s[rows, :] + hf_ref[rows, :]
            o_ref[rows, :] = (h * sg_ref[rows, :].astype(F32)).astype(o_ref.dtype)
            return carry
        lax.fori_loop(0, t_rows // LRU_TILE, out_body, 0)


def _lru_call(u, conv_w, conv_b, w_r, w_i, b_r, b_i, lam, *, reverse, zb=None, h_fwd=None):
    s = u.shape[0]
    n_chunks = s // LRU_T
    n_cb = D_MODEL // LRU_CW
    final = h_fwd is not None
    halo_blocks = LRU_T // SUBLANES
    n_halo = s // SUBLANES

    def chunk(i):
        return (n_chunks - 1 - i) if reverse else i

    in_specs = [
        pl.BlockSpec((SUBLANES, LRU_CW),
                     lambda n, i: (jnp.maximum(chunk(i) * halo_blocks - 1, 0), n)),
        pl.BlockSpec((LRU_T, LRU_CW), lambda n, i: (chunk(i), n)),
        pl.BlockSpec((SUBLANES, LRU_CW),
                     lambda n, i: (jnp.minimum((chunk(i) + 1) * halo_blocks, n_halo - 1), n)),
        pl.BlockSpec((4, LRU_CW), lambda n, i: (0, n)),
        pl.BlockSpec((1, LRU_CW), lambda n, i: (0, n)),
        pl.BlockSpec((LRU_CW // LRU_BLOCK_DIM, LRU_BLOCK_DIM, LRU_BLOCK_DIM), lambda n, i: (n, 0, 0)),
        pl.BlockSpec((LRU_CW // LRU_BLOCK_DIM, LRU_BLOCK_DIM, LRU_BLOCK_DIM), lambda n, i: (n, 0, 0)),
        pl.BlockSpec((1, LRU_CW), lambda n, i: (0, n)),
        pl.BlockSpec((1, LRU_CW), lambda n, i: (0, n)),
        pl.BlockSpec((1, LRU_CW), lambda n, i: (0, n)),
    ]
    args = [u, u, u, conv_w, conv_b, w_r, w_i, b_r, b_i, lam]
    scratch = [
        pltpu.VMEM((LRU_T + 2 * SUBLANES, LRU_CW), F32),
        pltpu.VMEM((LRU_T, LRU_CW), F32),
        pltpu.VMEM((LRU_T, LRU_CW), F32),
        pltpu.VMEM((LRU_T, LRU_CW), F32),
        pltpu.VMEM((LRU_T, LRU_CW), F32),
        pltpu.VMEM((LRU_T, LRU_CW), F32),
    ]
    if final:
        sg_col0 = B_SG_LRU // LRU_CW
        in_specs += [
            pl.BlockSpec((LRU_T, LRU_CW), lambda n, i: (chunk(i), sg_col0 + n)),
            pl.BlockSpec((LRU_T, LRU_CW), lambda n, i: (chunk(i), n)),
        ]
        args += [zb, h_fwd]
        scratch.append(pltpu.VMEM((LRU_T, LRU_CW), F32))
        out_dtype = BF16
    else:
        out_dtype = F32
    scratch.append(pltpu.VMEM((SUBLANES, LRU_CW), F32))

    return pl.pallas_call(
        functools.partial(_lru_kernel, reverse, final, n_chunks),
        out_shape=jax.ShapeDtypeStruct((s, D_MODEL), out_dtype),
        grid=(n_cb, n_chunks),
        in_specs=in_specs,
        out_specs=pl.BlockSpec((LRU_T, LRU_CW), lambda n, i: (chunk(i), n)),
        scratch_shapes=scratch,
        compiler_params=pltpu.CompilerParams(
            dimension_semantics=("parallel", "arbitrary"),
            vmem_limit_bytes=VMEM_LIMIT),
        name="lru_bwd" if reverse else "lru_fwd",
    )(*args)


ATT_QB = 2
ATT_TQ = ATT_QB * BLOCK


def _attn_kernel(n_steps, sink_ref, q_ref, kl_ref, kc_ref, kr_ref, vl_ref, vc_ref, vr_ref,
                 sg_ref, o_ref):
    step = pl.program_id(0)
    k_ext = jnp.concatenate([kl_ref[...], kc_ref[...], kr_ref[...]], axis=0)
    v_ext = jnp.concatenate([vl_ref[...], vc_ref[...], vr_ref[...]], axis=0)

    q_idx = lax.broadcasted_iota(jnp.int32, (BLOCK, 3 * BLOCK), 0)
    s_idx = lax.broadcasted_iota(jnp.int32, (BLOCK, 3 * BLOCK), 1)
    rel = s_idx - BLOCK - q_idx
    band_ok = (rel <= WINDOW) & (rel >= -WINDOW)
    head_of_row = lax.broadcasted_iota(jnp.int32, (GROUP * BLOCK, 1), 0) >> 7

    for qb in range(ATT_QB):
        ok = band_ok
        if qb == 0:
            ok = ok & ((s_idx >= BLOCK) | (step > 0))
        if qb == ATT_QB - 1:
            ok = ok & ((s_idx < 2 * BLOCK) | (step < n_steps - 1))
        bias = jnp.where(ok, 0.0, MASK_VALUE)
        bias = jnp.concatenate([bias] * GROUP, axis=0)
        q_rows = slice(qb * BLOCK, (qb + 1) * BLOCK)
        k_rows = slice(qb * BLOCK, (qb + 3) * BLOCK)
        for kh in range(N_KV_HEADS):
            kv_cols = slice(kh * HEAD_DIM, (kh + 1) * HEAD_DIM)
            q = jnp.concatenate(
                [q_ref[q_rows, (kh * GROUP + g) * HEAD_DIM:(kh * GROUP + g + 1) * HEAD_DIM]
                 for g in range(GROUP)], axis=0)
            scores = lax.dot_general(q, k_ext[k_rows, kv_cols], (((1,), (1,)), ((), ())),
                                     preferred_element_type=F32)
            scores = jnp.where(bias == 0.0, scores, MASK_VALUE)
            sink = jnp.zeros((GROUP * BLOCK, 1), F32)
            for g in range(GROUP):
                sink = jnp.where(head_of_row == g, sink_ref[kh * GROUP + g] * LOG2E, sink)
            m = jnp.maximum(jnp.max(scores, axis=-1, keepdims=True), sink)
            e = jnp.exp2(scores - m)
            denom = jnp.sum(e, axis=-1, keepdims=True) + jnp.exp2(sink - m)
            out = jnp.dot(e.astype(BF16), v_ext[k_rows, kv_cols],
                          preferred_element_type=F32) / denom
            for g in range(GROUP):
                cols = slice((kh * GROUP + g) * HEAD_DIM, (kh * GROUP + g + 1) * HEAD_DIM)
                gate = sg_ref[q_rows, cols].astype(F32)
                o_ref[q_rows, cols] = (out[g * BLOCK:(g + 1) * BLOCK, :] * gate).astype(o_ref.dtype)


def _attention(zb, sink):
    s = zb.shape[0]
    n_steps = s // ATT_TQ
    n_blocks = s // BLOCK
    aw = N_HEADS * HEAD_DIM

    def left(t):
        return jnp.maximum(t * ATT_QB - 1, 0)

    def right(t):
        return jnp.minimum((t + 1) * ATT_QB, n_blocks - 1)

    kcol = B_K // KV_WIDTH
    vcol = B_V // KV_WIDTH
    in_specs = [
        pl.BlockSpec(memory_space=pltpu.SMEM),
        pl.BlockSpec((ATT_TQ, aw), lambda t: (t, B_Q // aw)),
        pl.BlockSpec((BLOCK, KV_WIDTH), lambda t: (left(t), kcol)),
        pl.BlockSpec((ATT_TQ, KV_WIDTH), lambda t: (t, kcol)),
        pl.BlockSpec((BLOCK, KV_WIDTH), lambda t: (right(t), kcol)),
        pl.BlockSpec((BLOCK, KV_WIDTH), lambda t: (left(t), vcol)),
        pl.BlockSpec((ATT_TQ, KV_WIDTH), lambda t: (t, vcol)),
        pl.BlockSpec((BLOCK, KV_WIDTH), lambda t: (right(t), vcol)),
        pl.BlockSpec((ATT_TQ, aw), lambda t: (t, B_SG_ATT // aw)),
    ]
    return pl.pallas_call(
        functools.partial(_attn_kernel, n_steps),
        out_shape=jax.ShapeDtypeStruct((s, aw), BF16),
        grid=(n_steps,),
        in_specs=in_specs,
        out_specs=pl.BlockSpec((ATT_TQ, aw), lambda t: (t, 0)),
        compiler_params=pltpu.CompilerParams(
            dimension_semantics=("parallel",),
            vmem_limit_bytes=VMEM_LIMIT),
        name="attn",
    )(sink, zb, zb, zb, zb, zb, zb, zb, zb)


MERGE_TM = 256


def _merge_kernel(ya_ref, yb_ref, ga_ref, gb_ref, x_ref, wa_ref, wb_ref, wo_ref, nw_ref, o_ref):
    pa = jnp.dot(ya_ref[...], wa_ref[...], preferred_element_type=F32)
    pb = jnp.dot(yb_ref[...], wb_ref[...], preferred_element_type=F32)
    merged = ga_ref[...].astype(F32) * pa + gb_ref[...].astype(F32) * pb
    y = jnp.dot(merged.astype(BF16), wo_ref[...], preferred_element_type=F32)
    ms = jnp.mean(y * y, axis=-1, keepdims=True)
    o_ref[...] = x_ref[...] + y * lax.rsqrt(ms + NORM_EPS) * nw_ref[...]


def _merge(y_a, y_b, zb, x2, wa, wb, wo, norm_w):
    s = x2.shape[0]
    row = lambda i: (i, 0)
    const = lambda i: (0, 0)
    resident = functools.partial(pl.BlockSpec, pipeline_mode=pl.Buffered(1))
    return pl.pallas_call(
        _merge_kernel,
        out_shape=jax.ShapeDtypeStruct((s, D_MODEL), F32),
        grid=(s // MERGE_TM,),
        in_specs=[
            pl.BlockSpec((MERGE_TM, D_MODEL), row),
            pl.BlockSpec((MERGE_TM, D_MODEL), row),
            pl.BlockSpec((MERGE_TM, D_MODEL), lambda i: (i, B_GM_LRU // D_MODEL)),
            pl.BlockSpec((MERGE_TM, D_MODEL), lambda i: (i, B_GM_ATT // D_MODEL)),
            pl.BlockSpec((MERGE_TM, D_MODEL), row),
            resident((D_MODEL, D_MODEL), const),
            resident((D_MODEL, D_MODEL), const),
            resident((D_MODEL, D_MODEL), const),
            pl.BlockSpec((1, D_MODEL), const),
        ],
        out_specs=pl.BlockSpec((MERGE_TM, D_MODEL), row),
        compiler_params=pltpu.CompilerParams(
            dimension_semantics=("parallel",),
            vmem_limit_bytes=VMEM_LIMIT),
        name="merge",
    )(y_a, y_b, zb, zb, x2, wa, wb, wo, norm_w)


def _rope_packed(s):
    pos = jnp.arange(s, dtype=F32)
    inv_freq = ROPE_THETA ** (-jnp.arange(0, ROT_DIM, 2, dtype=F32) / ROT_DIM)
    ang = pos[:, None] * inv_freq[None, :]
    cos, sin = jnp.cos(ang), jnp.sin(ang)
    return jnp.concatenate([cos, cos, -sin, sin, jnp.zeros((s, LANES - 2 * ROT_DIM), F32)], axis=-1)


def _cat_in_weights(w):
    cols = lambda lo, n: w[:, lo:lo + n]
    return jnp.concatenate([
        cols(COL_U, 2048), cols(COL_G, 2048), cols(COL_GA, 2048), cols(COL_MA, 2048),
        cols(COL_MB, 2048), cols(COL_Q, 2048), cols(COL_K, 512), cols(COL_V, 512)],
        axis=1).astype(BF16)


def kernel(x, norm_pre_w, w_in, conv_w, conv_b, lru_w_r, lru_b_r, lru_w_i, lru_b_i,
           lru_lambda, attn_sink, w_proj_a, w_proj_b, w_out, norm_post_w):
    bsz, s, d = x.shape
    depth = w_in.shape[0]
    rope_packed = _rope_packed(s)
    outs = []
    for bi in range(bsz):
        xb = x[bi]
        for l in range(depth):
            u, zb = _inproj(xb, norm_pre_w[l][None, :], _cat_in_weights(w_in[l]), rope_packed)
            lru_args = (u, conv_w[l], conv_b[l][None, :])
            h_fwd = _lru_call(*lru_args, lru_w_r[l, 0].astype(BF16), lru_w_i[l, 0].astype(BF16),
                              lru_b_r[l, 0][None, :], lru_b_i[l, 0][None, :],
                              lru_lambda[l, 0][None, :], reverse=False)
            y_a = _lru_call(*lru_args, lru_w_r[l, 1].astype(BF16), lru_w_i[l, 1].astype(BF16),
                            lru_b_r[l, 1][None, :], lru_b_i[l, 1][None, :],
                            lru_lambda[l, 1][None, :], reverse=True, zb=zb, h_fwd=h_fwd)
            y_b = _attention(zb, attn_sink[l])
            xb = _merge(y_a, y_b, zb, xb, w_proj_a[l].astype(BF16), w_proj_b[l].astype(BF16),
                        w_out[l].astype(BF16), norm_post_w[l][None, :])
        outs.append(xb)
    return jnp.stack(outs, axis=0)
```

```python
import functools
import math

import jax
import jax.numpy as jnp
from jax import lax
from jax.experimental import pallas as pl
from jax.experimental.pallas import tpu as pltpu

F32 = jnp.float32
BF16 = jnp.bfloat16

D_MODEL = 2048
LRU_BLOCK_DIM = 128
LRU_BLOCKS = D_MODEL // LRU_BLOCK_DIM
LRU_C = 8.0
N_HEADS = 16
N_KV_HEADS = 4
GROUP = N_HEADS // N_KV_HEADS
HEAD_DIM = 128
KV_WIDTH = N_KV_HEADS * HEAD_DIM
BLOCK = 128
WINDOW = 128
ROPE_THETA = 500000.0
ROT_DIM = HEAD_DIM // 4
NORM_EPS = 1e-6
MASK_VALUE = -1e30
LOG2E = math.log2(math.e)

B_SG_LRU, B_SG_ATT, B_GM_LRU, B_GM_ATT, B_Q, B_K, B_V = (
    0, 2048, 4096, 6144, 8192, 10240, 10752)
B_WIDTH = 11264

SUBLANES = 8
LANES = 128
VMEM_LIMIT = 56 * 1024 * 1024


def _sigmoid(x):
    return 0.5 * jnp.tanh(0.5 * x) + 0.5


def _silu(x):
    hx = 0.5 * x
    return hx * jnp.tanh(hx) + hx


INPROJ_TM = 1024
INPROJ_TN = 1024
NORM_ROWS = 64
J_U, J_SILU, J_SIG, J_Q, J_KV, J_END = 0, 2, 6, 10, 12, 13


def _weight_block(j):
    return jnp.where(j < 4, j, jnp.where(j < 10, j + 3, j - 6))


def _rope_tables_from_packed(packed, scale):
    lane = lax.broadcasted_iota(jnp.int32, packed.shape, 1)
    in_rot = lane < ROT_DIM
    cos_t = jnp.where(in_rot, packed, 1.0) * scale
    sin_t = jnp.where(in_rot, pltpu.roll(packed, LANES - ROT_DIM, axis=1), 0.0) * scale
    return cos_t, sin_t


def _rope(t, cos_t, sin_t):
    half = ROT_DIM // 2
    lane = lax.broadcasted_iota(jnp.int32, t.shape, 1)
    partner = jnp.where(lane < half,
                        pltpu.roll(t, HEAD_DIM - half, axis=1),
                        pltpu.roll(t, half, axis=1))
    return t * cos_t + partner * sin_t


def _inproj_kernel(x_ref, nw_ref, w_ref, rope_ref, u_ref, b_ref, xn_ref):
    j = pl.program_id(1)

    @pl.when(j == 0)
    def _():
        def body(r, carry):
            rows = pl.ds(pl.multiple_of(r * NORM_ROWS, NORM_ROWS), NORM_ROWS)
            xv = x_ref[rows, :]
            ms = jnp.mean(xv * xv, axis=-1, keepdims=True)
            xn_ref[rows, :] = (xv * lax.rsqrt(ms + NORM_EPS) * nw_ref[...]).astype(BF16)
            return carry
        lax.fori_loop(0, INPROJ_TM // NORM_ROWS, body, 0)

    def project():
        return jnp.dot(xn_ref[...], w_ref[...], preferred_element_type=F32)

    @pl.when(j < J_SILU)
    def _():
        u_ref[...] = project()

    @pl.when((j >= J_SILU) & (j < J_SIG))
    def _():
        b_ref[...] = _silu(project()).astype(BF16)

    @pl.when((j >= J_SIG) & (j < J_Q))
    def _():
        b_ref[...] = _sigmoid(project()).astype(BF16)

    @pl.when((j >= J_Q) & (j < J_KV))
    def _():
        cos_t, sin_t = _rope_tables_from_packed(rope_ref[...], HEAD_DIM ** -0.5 * LOG2E)
        acc = project()
        for hb in range(INPROJ_TN // HEAD_DIM):
            cols = slice(hb * HEAD_DIM, (hb + 1) * HEAD_DIM)
            b_ref[:, cols] = _rope(acc[:, cols], cos_t, sin_t).astype(BF16)

    @pl.when(j == J_KV)
    def _():
        cos_t, sin_t = _rope_tables_from_packed(rope_ref[...], 1.0)
        acc = project()
        for hb in range(N_KV_HEADS):
            cols = slice(hb * HEAD_DIM, (hb + 1) * HEAD_DIM)
            b_ref[:, cols] = _rope(acc[:, cols], cos_t, sin_t).astype(BF16)
        b_ref[:, KV_WIDTH:] = acc[:, KV_WIDTH:].astype(BF16)


def _inproj(x2, norm_w, w_in_bf16, rope_packed):
    s = x2.shape[0]
    return pl.pallas_call(
        _inproj_kernel,
        out_shape=(jax.ShapeDtypeStruct((s, D_MODEL), F32),
                   jax.ShapeDtypeStruct((s, B_WIDTH), BF16)),
        grid=(s // INPROJ_TM, J_END),
        in_specs=[
            pl.BlockSpec((INPROJ_TM, D_MODEL), lambda i, j: (i, 0)),
            pl.BlockSpec((1, D_MODEL), lambda i, j: (0, 0)),
            pl.BlockSpec((D_MODEL, INPROJ_TN), lambda i, j: (0, _weight_block(j))),
            pl.BlockSpec((INPROJ_TM, LANES), lambda i, j: (i, 0)),
        ],
        out_specs=(
            pl.BlockSpec((INPROJ_TM, INPROJ_TN), lambda i, j: (i, jnp.minimum(j, J_SILU - 1))),
            pl.BlockSpec((INPROJ_TM, INPROJ_TN), lambda i, j: (i, jnp.maximum(j - J_SILU, 0))),
        ),
        scratch_shapes=[pltpu.VMEM((INPROJ_TM, D_MODEL), BF16)],
        compiler_params=pltpu.CompilerParams(
            dimension_semantics=("parallel", "arbitrary"),
            vmem_limit_bytes=VMEM_LIMIT),
        name="inproj",
    )(x2, norm_w, w_in_bf16, rope_packed)


LRU_SUB = 128
LRU_ROWS = SUBLANES * LRU_SUB
LRU_CW = 256
LRU_NK = LRU_CW // LRU_BLOCK_DIM
CONV_LEFT = 2
U_ROWS = (LRU_SUB + 3) * SUBLANES
CONV_TILE = 128
GATE_COLS = 4 * LRU_BLOCK_DIM
LRU_UNROLL = 4


def _sqrt_nonneg(x):
    return jnp.where(x > 0.0, x * lax.rsqrt(x), 0.0)


def _lru_local_kernel(n_chunks, up_ref, uc_ref, un_ref, cw_ref, cb_ref, w_ref, bias_ref, lam_ref,
                      h_ref, pf_ref, pb_ref, ef_ref, pef_ref, eb_ref, peb_ref,
                      u_s, uc_s, z_s, hf_s, hb_s, pf_s, pb_s):
    c = pl.program_id(1)
    sub = lax.broadcasted_iota(jnp.int32, (SUBLANES, LANES), 0)
    first_slab = CONV_LEFT * SUBLANES

    for k in range(LRU_NK):
        cols = slice(k * LANES, (k + 1) * LANES)
        for s in range(SUBLANES):
            u_s[k, pl.ds(first_slab + s, LRU_SUB, stride=SUBLANES), :] = (
                uc_ref[s * LRU_SUB:(s + 1) * LRU_SUB, cols])
        prev1 = jnp.where(c > 0, up_ref[SUBLANES - 1:SUBLANES, cols], 0.0)
        prev2 = jnp.where(c > 0, up_ref[SUBLANES - 2:SUBLANES - 1, cols], 0.0)
        nxt = jnp.where(c < n_chunks - 1, un_ref[0:1, cols], 0.0)
        last1 = u_s[k, pl.ds(first_slab + (LRU_SUB - 1) * SUBLANES, SUBLANES), :]
        last2 = u_s[k, pl.ds(first_slab + (LRU_SUB - 2) * SUBLANES, SUBLANES), :]
        head = u_s[k, pl.ds(first_slab, SUBLANES), :]
        u_s[k, pl.ds(SUBLANES, SUBLANES), :] = jnp.where(sub == 0, prev1, pltpu.roll(last1, 1, axis=0))
        u_s[k, pl.ds(0, SUBLANES), :] = jnp.where(sub == 0, prev2, pltpu.roll(last2, 1, axis=0))
        u_s[k, pl.ds(first_slab + LRU_SUB * SUBLANES, SUBLANES), :] = jnp.where(
            sub == SUBLANES - 1, nxt, pltpu.roll(head, SUBLANES - 1, axis=0))

        for t0 in range(0, LRU_ROWS, CONV_TILE):
            y = cb_ref[:, cols]
            for tap in range(4):
                lo = t0 + tap * SUBLANES
                y = y + u_s[k, lo:lo + CONV_TILE, :] * cw_ref[tap:tap + 1, cols]
            uc_s[k, t0:t0 + CONV_TILE, :] = y

        z_s[k] = jnp.dot(uc_s[k].astype(BF16), w_ref[k], preferred_element_type=F32)

    def quarter_decay(row, cols):
        nl = -lam_ref[row:row + 1, cols]
        softplus = jnp.maximum(nl, 0.0) + jnp.log1p(jnp.exp(-jnp.abs(nl)))
        return jnp.broadcast_to(-0.25 * LRU_C * softplus, (SUBLANES, LANES))

    consts = []
    for k in range(LRU_NK):
        cols = slice(k * LANES, (k + 1) * LANES)
        per_dir = []
        for d in range(2):
            per_dir.append((
                jnp.broadcast_to(bias_ref[2 * d:2 * d + 1, cols], (SUBLANES, LANES)),
                jnp.broadcast_to(bias_ref[2 * d + 1:2 * d + 2, cols], (SUBLANES, LANES)),
                quarter_decay(d, cols)))
        consts.append(per_dir)

    def step(k, d, r, h, p):
        rows = pl.ds(pl.multiple_of(r * SUBLANES, SUBLANES), SUBLANES)
        br_h, bi_h, dq = consts[k][d]
        zr = z_s[k, rows, 2 * d * LANES:(2 * d + 1) * LANES]
        zi = z_s[k, rows, (2 * d + 1) * LANES:(2 * d + 2) * LANES]
        u = uc_s[k, rows, :]
        tr = jnp.tanh(zr + br_h)
        th = jnp.tanh(tr * dq + dq)
        inv = 1.0 / (1.0 - th)
        a = (1.0 + th) * inv
        root = _sqrt_nonneg(-th)
        ti = jnp.tanh(zi + bi_h)
        b = (root * inv) * ((ti + 1.0) * u)
        h = a * h + b
        p = a * p
        if d == 0:
            hf_s[k, rows, :] = h
            pf_s[k, rows, :] = p
        else:
            hb_s[k, rows, :] = h
            pb_s[k, rows, :] = p
        return h, p

    def body(it, carry):
        out = []
        for k in range(LRU_NK):
            hf, pf, hb, pb = carry[4 * k:4 * k + 4]
            hf, pf = step(k, 0, it, hf, pf)
            hb, pb = step(k, 1, LRU_SUB - 1 - it, hb, pb)
            out += [hf, pf, hb, pb]
        return tuple(out)

    zero = jnp.zeros((SUBLANES, LANES), F32)
    one = jnp.ones((SUBLANES, LANES), F32)
    final = lax.fori_loop(0, LRU_SUB, body, (zero, one, zero, one) * LRU_NK, unroll=LRU_UNROLL)

    for k in range(LRU_NK):
        cols = slice(k * LANES, (k + 1) * LANES)
        hf, pf, hb, pb = final[4 * k:4 * k + 4]
        ef_ref[:, cols] = hf
        pef_ref[:, cols] = pf
        eb_ref[:, cols] = hb
        peb_ref[:, cols] = pb
        for s in range(SUBLANES):
            rows = slice(s * LRU_SUB, (s + 1) * LRU_SUB)
            sel = pl.ds(s, LRU_SUB, stride=SUBLANES)
            h_ref[rows, cols] = hf_s[k, sel, :] + hb_s[k, sel, :]
            pf_ref[rows, cols] = pf_s[k, sel, :].astype(BF16)
            pb_ref[rows, cols] = pb_s[k, sel, :].astype(BF16)


def _lru_local(u, conv_w, conv_b, w_gates, bias_half, lam):
    s = u.shape[0]
    n_chunks = s // LRU_ROWS
    n_cb = D_MODEL // LRU_CW
    halo = LRU_ROWS // SUBLANES
    n_halo = s // SUBLANES
    n_sub = s // LRU_SUB
    big = lambda n, c: (c, n)
    scan_scratch = pltpu.VMEM((LRU_NK, LRU_ROWS, LANES), F32)
    return pl.pallas_call(
        functools.partial(_lru_local_kernel, n_chunks),
        out_shape=(jax.ShapeDtypeStruct((s, D_MODEL), F32),
                   jax.ShapeDtypeStruct((s, D_MODEL), BF16),
                   jax.ShapeDtypeStruct((s, D_MODEL), BF16),
                   jax.ShapeDtypeStruct((n_sub, D_MODEL), F32),
                   jax.ShapeDtypeStruct((n_sub, D_MODEL), F32),
                   jax.ShapeDtypeStruct((n_sub, D_MODEL), F32),
                   jax.ShapeDtypeStruct((n_sub, D_MODEL), F32)),
        grid=(n_cb, n_chunks),
        in_specs=[
            pl.BlockSpec((SUBLANES, LRU_CW), lambda n, c: (jnp.maximum(c * halo - 1, 0), n)),
            pl.BlockSpec((LRU_ROWS, LRU_CW), big),
            pl.BlockSpec((SUBLANES, LRU_CW), lambda n, c: (jnp.minimum((c + 1) * halo, n_halo - 1), n)),
            pl.BlockSpec((4, LRU_CW), lambda n, c: (0, n)),
            pl.BlockSpec((1, LRU_CW), lambda n, c: (0, n)),
            pl.BlockSpec((LRU_NK, LRU_BLOCK_DIM, GATE_COLS), lambda n, c: (n, 0, 0)),
            pl.BlockSpec((4, LRU_CW), lambda n, c: (0, n)),
            pl.BlockSpec((2, LRU_CW), lambda n, c: (0, n)),
        ],
        out_specs=(
            pl.BlockSpec((LRU_ROWS, LRU_CW), big),
            pl.BlockSpec((LRU_ROWS, LRU_CW), big),
            pl.BlockSpec((LRU_ROWS, LRU_CW), big),
            pl.BlockSpec((SUBLANES, LRU_CW), big),
            pl.BlockSpec((SUBLANES, LRU_CW), big),
            pl.BlockSpec((SUBLANES, LRU_CW), big),
            pl.BlockSpec((SUBLANES, LRU_CW), big),
        ),
        scratch_shapes=[
            pltpu.VMEM((LRU_NK, U_ROWS, LANES), F32),
            scan_scratch,
            pltpu.VMEM((LRU_NK, LRU_ROWS, GATE_COLS), F32),
            scan_scratch, scan_scratch, scan_scratch, scan_scratch,
        ],
        compiler_params=pltpu.CompilerParams(
            dimension_semantics=("parallel", "parallel"),
            vmem_limit_bytes=VMEM_LIMIT),
        name="lru_local",
    )(u, u, u, conv_w, conv_b, w_gates, bias_half, lam)


def _lru_carry_kernel(ef_ref, pef_ref, eb_ref, peb_ref, cf_ref, cbk_ref):
    n_sub = ef_ref.shape[0]

    def fwd(g, carry):
        row = pl.ds(g, 1)
        cf_ref[row, :] = carry
        return ef_ref[row, :] + pef_ref[row, :] * carry

    def bwd(i, carry):
        row = pl.ds(n_sub - 1 - i, 1)
        cbk_ref[row, :] = carry
        return eb_ref[row, :] + peb_ref[row, :] * carry

    zero = jnp.zeros((1, ef_ref.shape[1]), F32)
    lax.fori_loop(0, n_sub, fwd, zero)
    lax.fori_loop(0, n_sub, bwd, zero)


def _lru_carry(ef, pef, eb, peb):
    shape = jax.ShapeDtypeStruct(ef.shape, F32)
    return pl.pallas_call(
        _lru_carry_kernel,
        out_shape=(shape, shape),
        name="lru_carry",
    )(ef, pef, eb, peb)


ATT_QB = 2
ATT_TQ = ATT_QB * BLOCK


def _attn_kernel(n_steps, sink_ref, q_ref, kl_ref, kc_ref, kr_ref, vl_ref, vc_ref, vr_ref,
                 sg_ref, o_ref):
    step = pl.program_id(0)
    k_ext = jnp.concatenate([kl_ref[...], kc_ref[...], kr_ref[...]], axis=0)
    v_ext = jnp.concatenate([vl_ref[...], vc_ref[...], vr_ref[...]], axis=0)

    q_idx = lax.broadcasted_iota(jnp.int32, (BLOCK, 3 * BLOCK), 0)
    s_idx = lax.broadcasted_iota(jnp.int32, (BLOCK, 3 * BLOCK), 1)
    rel = s_idx - BLOCK - q_idx
    band_ok = (rel <= WINDOW) & (rel >= -WINDOW)
    head_of_row = lax.broadcasted_iota(jnp.int32, (GROUP * BLOCK, 1), 0) >> 7

    for qb in range(ATT_QB):
        ok = band_ok
        if qb == 0:
            ok = ok & ((s_idx >= BLOCK) | (step > 0))
        if qb == ATT_QB - 1:
            ok = ok & ((s_idx < 2 * BLOCK) | (step < n_steps - 1))
        bias = jnp.where(ok, 0.0, MASK_VALUE)
        bias = jnp.concatenate([bias] * GROUP, axis=0)
        q_rows = slice(qb * BLOCK, (qb + 1) * BLOCK)
        k_rows = slice(qb * BLOCK, (qb + 3) * BLOCK)
        for kh in range(N_KV_HEADS):
            kv_cols = slice(kh * HEAD_DIM, (kh + 1) * HEAD_DIM)
            q = jnp.concatenate(
                [q_ref[q_rows, (kh * GROUP + g) * HEAD_DIM:(kh * GROUP + g + 1) * HEAD_DIM]
                 for g in range(GROUP)], axis=0)
            scores = lax.dot_general(q, k_ext[k_rows, kv_cols], (((1,), (1,)), ((), ())),
                                     preferred_element_type=F32)
            scores = jnp.where(bias == 0.0, scores, MASK_VALUE)
            sink = jnp.zeros((GROUP * BLOCK, 1), F32)
            for g in range(GROUP):
                sink = jnp.where(head_of_row == g, sink_ref[kh * GROUP + g] * LOG2E, sink)
            m = jnp.maximum(jnp.max(scores, axis=-1, keepdims=True), sink)
            e = jnp.exp2(scores - m)
            denom = jnp.sum(e, axis=-1, keepdims=True) + jnp.exp2(sink - m)
            out = jnp.dot(e.astype(BF16), v_ext[k_rows, kv_cols],
                          preferred_element_type=F32) / denom
            for g in range(GROUP):
                cols = slice((kh * GROUP + g) * HEAD_DIM, (kh * GROUP + g + 1) * HEAD_DIM)
                gate = sg_ref[q_rows, cols].astype(F32)
                o_ref[q_rows, cols] = (out[g * BLOCK:(g + 1) * BLOCK, :] * gate).astype(o_ref.dtype)


def _attention(zb, sink):
    s = zb.shape[0]
    n_steps = s // ATT_TQ
    n_blocks = s // BLOCK
    aw = N_HEADS * HEAD_DIM

    def left(t):
        return jnp.maximum(t * ATT_QB - 1, 0)

    def right(t):
        return jnp.minimum((t + 1) * ATT_QB, n_blocks - 1)

    kcol = B_K // KV_WIDTH
    vcol = B_V // KV_WIDTH
    in_specs = [
        pl.BlockSpec(memory_space=pltpu.SMEM),
        pl.BlockSpec((ATT_TQ, aw), lambda t: (t, B_Q // aw)),
        pl.BlockSpec((BLOCK, KV_WIDTH), lambda t: (left(t), kcol)),
        pl.BlockSpec((ATT_TQ, KV_WIDTH), lambda t: (t, kcol)),
        pl.BlockSpec((BLOCK, KV_WIDTH), lambda t: (right(t), kcol)),
        pl.BlockSpec((BLOCK, KV_WIDTH), lambda t: (left(t), vcol)),
        pl.BlockSpec((ATT_TQ, KV_WIDTH), lambda t: (t, vcol)),
        pl.BlockSpec((BLOCK, KV_WIDTH), lambda t: (right(t), vcol)),
        pl.BlockSpec((ATT_TQ, aw), lambda t: (t, B_SG_ATT // aw)),
    ]
    return pl.pallas_call(
        functools.partial(_attn_kernel, n_steps),
        out_shape=jax.ShapeDtypeStruct((s, aw), BF16),
        grid=(n_steps,),
        in_specs=in_specs,
        out_specs=pl.BlockSpec((ATT_TQ, aw), lambda t: (t, 0)),
        compiler_params=pltpu.CompilerParams(
            dimension_semantics=("parallel",),
            vmem_limit_bytes=VMEM_LIMIT),
        name="attn",
    )(sink, zb, zb, zb, zb, zb, zb, zb, zb)


MERGE_TM = 256


def _merge_kernel(h_ref, pf_ref, pb_ref, sg_ref, yb_ref, ga_ref, gb_ref, cf_ref, cbk_ref,
                  wa_ref, wb_ref, o_ref):
    i = pl.program_id(0)
    parts = []
    for q in range(MERGE_TM // LRU_SUB):
        g = i * (MERGE_TM // LRU_SUB) + q
        rows = slice(q * LRU_SUB, (q + 1) * LRU_SUB)
        h = (h_ref[rows, :]
             + pf_ref[rows, :].astype(F32) * cf_ref[pl.ds(g, 1), :]
             + pb_ref[rows, :].astype(F32) * cbk_ref[pl.ds(g, 1), :])
        parts.append((h * sg_ref[rows, :].astype(F32)).astype(BF16))
    y_a = jnp.concatenate(parts, axis=0)
    pa = jnp.dot(y_a, wa_ref[...], preferred_element_type=F32)
    pb = jnp.dot(yb_ref[...], wb_ref[...], preferred_element_type=F32)
    o_ref[...] = (ga_ref[...].astype(F32) * pa + gb_ref[...].astype(F32) * pb).astype(BF16)


def _merge(h_loc, p_f, p_b, y_b, zb, c_f, c_b, wa, wb):
    s = h_loc.shape[0]
    row = lambda i: (i, 0)
    const = lambda i: (0, 0)
    resident = functools.partial(pl.BlockSpec, pipeline_mode=pl.Buffered(1))
    tile = (MERGE_TM, D_MODEL)
    return pl.pallas_call(
        _merge_kernel,
        out_shape=jax.ShapeDtypeStruct((s, D_MODEL), BF16),
        grid=(s // MERGE_TM,),
        in_specs=[
            pl.BlockSpec(tile, row),
            pl.BlockSpec(tile, row),
            pl.BlockSpec(tile, row),
            pl.BlockSpec(tile, lambda i: (i, B_SG_LRU // D_MODEL)),
            pl.BlockSpec(tile, row),
            pl.BlockSpec(tile, lambda i: (i, B_GM_LRU // D_MODEL)),
            pl.BlockSpec(tile, lambda i: (i, B_GM_ATT // D_MODEL)),
            resident(c_f.shape, const),
            resident(c_b.shape, const),
            resident((D_MODEL, D_MODEL), const),
            resident((D_MODEL, D_MODEL), const),
        ],
        out_specs=pl.BlockSpec(tile, row),
        compiler_params=pltpu.CompilerParams(
            dimension_semantics=("parallel",),
            vmem_limit_bytes=VMEM_LIMIT),
        name="merge",
    )(h_loc, p_f, p_b, zb, y_b, zb, zb, c_f, c_b, wa, wb)


OUT_TM = 512


def _out_kernel(m_ref, x_ref, wo_ref, nw_ref, o_ref):
    y = jnp.dot(m_ref[...], wo_ref[...], preferred_element_type=F32)
    ms = jnp.mean(y * y, axis=-1, keepdims=True)
    o_ref[...] = x_ref[...] + y * lax.rsqrt(ms + NORM_EPS) * nw_ref[...]


def _out_proj(merged, x2, wo, norm_w):
    s = x2.shape[0]
    row = lambda i: (i, 0)
    const = lambda i: (0, 0)
    return pl.pallas_call(
        _out_kernel,
        out_shape=jax.ShapeDtypeStruct((s, D_MODEL), F32),
        grid=(s // OUT_TM,),
        in_specs=[
            pl.BlockSpec((OUT_TM, D_MODEL), row),
            pl.BlockSpec((OUT_TM, D_MODEL), row),
            pl.BlockSpec((D_MODEL, D_MODEL), const, pipeline_mode=pl.Buffered(1)),
            pl.BlockSpec((1, D_MODEL), const),
        ],
        out_specs=pl.BlockSpec((OUT_TM, D_MODEL), row),
        compiler_params=pltpu.CompilerParams(
            dimension_semantics=("parallel",),
            vmem_limit_bytes=VMEM_LIMIT),
        name="out_proj",
    )(merged, x2, wo, norm_w)


def _rope_packed(s):
    pos = jnp.arange(s, dtype=F32)
    inv_freq = ROPE_THETA ** (-jnp.arange(0, ROT_DIM, 2, dtype=F32) / ROT_DIM)
    ang = inv_freq[:, None] * pos[None, :]
    cos, sin = jnp.cos(ang), jnp.sin(ang)
    packed = jnp.concatenate([cos, cos, -sin, sin, jnp.zeros((LANES - 2 * ROT_DIM, s), F32)], axis=0)
    return packed.T


def _gate_weights(w_r, w_i):
    w = jnp.concatenate([w_r[0], w_i[0], w_r[1], w_i[1]], axis=-1)
    return (0.5 * w).astype(BF16)


def kernel(x, norm_pre_w, w_in, conv_w, conv_b, lru_w_r, lru_b_r, lru_w_i, lru_b_i,
           lru_lambda, attn_sink, w_proj_a, w_proj_b, w_out, norm_post_w):
    bsz, s, d = x.shape
    depth = w_in.shape[0]
    rope_packed = _rope_packed(s)
    outs = []
    for bi in range(bsz):
        xb = x[bi]
        for l in range(depth):
            u, zb = _inproj(xb, norm_pre_w[l][None, :], w_in[l].astype(BF16), rope_packed)
            bias_half = 0.5 * jnp.stack([lru_b_r[l, 0], lru_b_i[l, 0], lru_b_r[l, 1], lru_b_i[l, 1]])
            h_loc, p_f, p_b, ef, pef, eb, peb = _lru_local(
                u, conv_w[l], conv_b[l][None, :], _gate_weights(lru_w_r[l], lru_w_i[l]),
                bias_half, lru_lambda[l])
            c_f, c_b = _lru_carry(ef, pef, eb, peb)
            y_b = _attention(zb, attn_sink[l])
            merged = _merge(h_loc, p_f, p_b, y_b, zb, c_f, c_b,
                            w_proj_a[l].astype(BF16), w_proj_b[l].astype(BF16))
            xb = _out_proj(merged, xb, w_out[l].astype(BF16), norm_post_w[l][None, :])
        outs.append(xb)
    return jnp.stack(outs, axis=0)
```

```python
import functools
import math

import jax
import jax.numpy as jnp
from jax import lax
from jax.experimental import pallas as pl
from jax.experimental.pallas import tpu as pltpu

F32 = jnp.float32
BF16 = jnp.bfloat16

D_MODEL = 2048
LRU_BLOCK_DIM = 128
LRU_BLOCKS = D_MODEL // LRU_BLOCK_DIM
LRU_C = 8.0
N_HEADS = 16
N_KV_HEADS = 4
GROUP = N_HEADS // N_KV_HEADS
HEAD_DIM = 128
KV_WIDTH = N_KV_HEADS * HEAD_DIM
BLOCK = 128
WINDOW = 128
ROPE_THETA = 500000.0
ROT_DIM = HEAD_DIM // 4
NORM_EPS = 1e-6
MASK_VALUE = -1e30
LOG2E = math.log2(math.e)

B_SG_LRU, B_SG_ATT, B_GM_LRU, B_GM_ATT, B_Q, B_K, B_V = (
    0, 2048, 4096, 6144, 8192, 10240, 10752)
B_WIDTH = 11264

SUBLANES = 8
BF16_SUBLANES = 16
LANES = 128
VMEM_LIMIT = 56 * 1024 * 1024


def _sigmoid(x):
    return 0.5 * jnp.tanh(0.5 * x) + 0.5


def _silu(x):
    hx = 0.5 * x
    return hx * jnp.tanh(hx) + hx


def _sqrt_nonneg(x):
    return jnp.where(x > 0.0, x * lax.rsqrt(x), 0.0)


INPROJ_TM = 1024
INPROJ_TN = 1024
NORM_ROWS = 64
XN_ROWS = INPROJ_TM + BF16_SUBLANES
J_U, J_SILU, J_SIG, J_Q, J_KV, J_END = 0, 2, 6, 10, 12, 13

LRU_SUB = 128
LRU_CW = 256
LRU_NK = LRU_CW // LRU_BLOCK_DIM
LRU_STEPS = D_MODEL // LRU_CW
SLABS_PER_J = INPROJ_TN // LANES
CONV_LEFT = 2
U_ROWS = (LRU_SUB + 3) * SUBLANES
CONV_TILE = 128
GATE_COLS = 4 * LRU_BLOCK_DIM
P_CONV_W, P_CONV_B, P_BIAS, P_LAMBDA, P_ROWS = 0, 4, 5, 9, 16


def _weight_block(j):
    return jnp.where(j < 4, j, jnp.where(j < 10, j + 3, j - 6))


def _rope_tables(freq_ref, rc_ref, rs_ref, chunk, scale):
    lane = lax.broadcasted_iota(jnp.int32, (1, LANES), 1)
    base = (chunk * INPROJ_TM).astype(F32) * freq_ref[...]
    cb, sb = jnp.cos(base) * scale, jnp.sin(base) * scale
    sign = jnp.where(lane < ROT_DIM // 2, -1.0, jnp.where(lane < ROT_DIM, 1.0, 0.0))
    rc, rs = rc_ref[...], rs_ref[...]
    cos_t = cb * rc - sb * rs
    sin_t = (sign * sb) * rc + (sign * cb) * rs
    return cos_t, sin_t


def _rope(t, cos_t, sin_t):
    half = ROT_DIM // 2
    lane = lax.broadcasted_iota(jnp.int32, t.shape, 1)
    partner = jnp.where(lane < half,
                        pltpu.roll(t, HEAD_DIM - half, axis=1),
                        pltpu.roll(t, half, axis=1))
    return t * cos_t + partner * sin_t


def _lru_conv(kp, lp_ref, u_s, uc_s, ucb_s):
    for k in range(LRU_NK):
        kk = kp * LRU_NK + k
        par = lp_ref.at[kk]
        u_k = u_s.at[kk]
        for t0 in range(0, INPROJ_TM, CONV_TILE):
            y = par[P_CONV_B:P_CONV_B + 1, :]
            for tap in range(4):
                lo = t0 + tap * SUBLANES
                y = y + u_k[lo:lo + CONV_TILE, :] * par[P_CONV_W + tap:P_CONV_W + tap + 1, :]
            uc_s[k, t0:t0 + CONV_TILE, :] = y
            ucb_s[k, t0:t0 + CONV_TILE, :] = y.astype(BF16)


def _lru_gates(kp, wg_ref, ucb_s, z_s):
    zrf_s, zif_s, zrb_s, zib_s = z_s
    for k in range(LRU_NK):
        z = jnp.dot(ucb_s[k], wg_ref[kp * LRU_NK + k], preferred_element_type=F32)
        zrf_s[k] = z[:, 0 * LANES:1 * LANES]
        zif_s[k] = z[:, 1 * LANES:2 * LANES]
        zrb_s[k] = z[:, 2 * LANES:3 * LANES]
        zib_s[k] = z[:, 3 * LANES:4 * LANES]


def _lru_scan(kp, lp_ref, uc_s, z_s, h_ref, pf_ref, pb_ref, ef_ref, pef_ref, eb_ref, peb_ref):
    zrf_s, zif_s, zrb_s, zib_s = z_s
    consts = []
    for k in range(LRU_NK):
        par = lp_ref.at[kp * LRU_NK + k]
        per_dir = []
        for d in range(2):
            nl = -par[P_LAMBDA + d:P_LAMBDA + d + 1, :]
            softplus = jnp.maximum(nl, 0.0) + jnp.log1p(jnp.exp(-jnp.abs(nl)))
            per_dir.append((
                jnp.broadcast_to(par[P_BIAS + 2 * d:P_BIAS + 2 * d + 1, :], (SUBLANES, LANES)),
                jnp.broadcast_to(par[P_BIAS + 2 * d + 1:P_BIAS + 2 * d + 2, :], (SUBLANES, LANES)),
                jnp.broadcast_to(-0.25 * LRU_C * softplus, (SUBLANES, LANES))))
        consts.append(per_dir)

    def step(k, d, r, h, p):
        rows = slice(r * SUBLANES, (r + 1) * SUBLANES)
        zr_s, zi_s = (zrf_s, zif_s) if d == 0 else (zrb_s, zib_s)
        br_h, bi_h, dq = consts[k][d]
        tr = jnp.tanh(zr_s[k, rows, :] + br_h)
        th = jnp.tanh(tr * dq + dq)
        inv = 1.0 / (1.0 - th)
        a = (1.0 + th) * inv
        root = _sqrt_nonneg(-th)
        ti = jnp.tanh(zi_s[k, rows, :] + bi_h)
        b = (root * inv) * ((ti + 1.0) * uc_s[k, rows, :])
        h = a * h + b
        p = a * p
        zr_s[k, rows, :] = h
        zi_s[k, rows, :] = p
        return h, p

    zero = jnp.zeros((SUBLANES, LANES), F32)
    one = jnp.ones((SUBLANES, LANES), F32)
    state = [[zero, one, zero, one] for _ in range(LRU_NK)]
    for it in range(LRU_SUB):
        for k in range(LRU_NK):
            hf, pf, hb, pb = state[k]
            hf, pf = step(k, 0, it, hf, pf)
            hb, pb = step(k, 1, LRU_SUB - 1 - it, hb, pb)
            state[k] = [hf, pf, hb, pb]

    for k in range(LRU_NK):
        cols = slice(k * LANES, (k + 1) * LANES)
        hf, pf, hb, pb = state[k]
        ef_ref[:, cols] = hf
        pef_ref[:, cols] = pf
        eb_ref[:, cols] = hb
        peb_ref[:, cols] = pb
        for s in range(SUBLANES):
            rows = slice(s * LRU_SUB, (s + 1) * LRU_SUB)
            sel = pl.ds(s, LRU_SUB, stride=SUBLANES)
            h_ref[rows, cols] = zrf_s[k, sel, :] + zrb_s[k, sel, :]
            pf_ref[rows, cols] = zif_s[k, sel, :].astype(BF16)
            pb_ref[rows, cols] = zib_s[k, sel, :].astype(BF16)


def _inproj_kernel(n_chunks, x_ref, xh_ref, nw_ref, w_ref, freq_ref, rc_ref, rs_ref, lp_ref, wg_ref,
                   b_ref, h_ref, pf_ref, pb_ref, ef_ref, pef_ref, eb_ref, peb_ref,
                   x_buf, x_sem, xn_ref, u_s, uprev_s, uc_s, ucb_s, zrf_s, zif_s, zrb_s, zib_s):
    i = pl.program_id(0)
    j = pl.program_id(1)
    z_s = (zrf_s, zif_s, zrb_s, zib_s)
    lru_outs = (h_ref, pf_ref, pb_ref, ef_ref, pef_ref, eb_ref, peb_ref)

    def x_copy(chunk):
        rows = pl.ds(pl.multiple_of(chunk * INPROJ_TM, INPROJ_TM), INPROJ_TM)
        return pltpu.make_async_copy(x_ref.at[rows, :], x_buf, x_sem)

    @pl.when((i == 0) & (j == 0))
    def _():
        x_copy(0).start()

    @pl.when((j == 1) & (i < n_chunks - 1))
    def _():
        x_copy(i + 1).start()

    def project_with_lru(epilogue):
        kp = j - J_SILU
        _lru_gates(kp, wg_ref, ucb_s, z_s)
        b_ref[...] = epilogue(project()).astype(BF16)
        _lru_scan(kp, lp_ref, uc_s, z_s, *lru_outs)
        _lru_conv(jnp.minimum(kp + 1, LRU_STEPS - 1), lp_ref, u_s, uc_s, ucb_s)

    def normalise(xv):
        ms = jnp.mean(xv * xv, axis=-1, keepdims=True)
        return (xv * lax.rsqrt(ms + NORM_EPS) * nw_ref[...]).astype(BF16)

    @pl.when(j == 0)
    def _():
        x_copy(i).wait()

        def body(r, carry):
            rows = pl.ds(pl.multiple_of(r * NORM_ROWS, NORM_ROWS), NORM_ROWS)
            xn_ref[rows, :] = normalise(x_buf[rows, :])
            return carry
        lax.fori_loop(0, INPROJ_TM // NORM_ROWS, body, 0)
        xn_ref[INPROJ_TM:, :] = normalise(xh_ref[...])

    def project():
        return jnp.dot(xn_ref[0:INPROJ_TM, :], w_ref[...], preferred_element_type=F32)

    @pl.when(j < J_SILU)
    def _():
        acc = jnp.dot(xn_ref[...], w_ref[...], preferred_element_type=F32)
        sub = lax.broadcasted_iota(jnp.int32, (SUBLANES, LANES), 0)
        first_slab = CONV_LEFT * SUBLANES
        for k in range(SLABS_PER_J):
            kk = j * SLABS_PER_J + k
            cols = slice(k * LANES, (k + 1) * LANES)
            u_k = u_s.at[kk]
            for s in range(SUBLANES):
                u_k[pl.ds(first_slab + s, LRU_SUB, stride=SUBLANES), :] = (
                    acc[s * LRU_SUB:(s + 1) * LRU_SUB, cols])
            prev = uprev_s[kk]
            prev1 = jnp.where(i > 0, prev[SUBLANES - 1:SUBLANES, :], 0.0)
            prev2 = jnp.where(i > 0, prev[SUBLANES - 2:SUBLANES - 1, :], 0.0)
            nxt = jnp.where(i < n_chunks - 1, acc[INPROJ_TM:INPROJ_TM + 1, cols], 0.0)
            last1 = u_k[pl.ds(first_slab + (LRU_SUB - 1) * SUBLANES, SUBLANES), :]
            last2 = u_k[pl.ds(first_slab + (LRU_SUB - 2) * SUBLANES, SUBLANES), :]
            head = u_k[pl.ds(first_slab, SUBLANES), :]
            u_k[pl.ds(SUBLANES, SUBLANES), :] = jnp.where(sub == 0, prev1, pltpu.roll(last1, 1, axis=0))
            u_k[pl.ds(0, SUBLANES), :] = jnp.where(sub == 0, prev2, pltpu.roll(last2, 1, axis=0))
            u_k[pl.ds(first_slab + LRU_SUB * SUBLANES, SUBLANES), :] = jnp.where(
                sub == SUBLANES - 1, nxt, pltpu.roll(head, SUBLANES - 1, axis=0))
            uprev_s[kk] = acc[INPROJ_TM - SUBLANES:INPROJ_TM, cols]
        _lru_conv(0, lp_ref, u_s, uc_s, ucb_s)

    @pl.when((j >= J_SILU) & (j < J_SIG))
    def _():
        project_with_lru(_silu)

    @pl.when((j >= J_SIG) & (j < J_Q))
    def _():
        project_with_lru(_sigmoid)

    @pl.when((j >= J_Q) & (j < J_KV))
    def _():
        cos_t, sin_t = _rope_tables(freq_ref, rc_ref, rs_ref, i, HEAD_DIM ** -0.5 * LOG2E)
        acc = project()
        for hb in range(INPROJ_TN // HEAD_DIM):
            cols = slice(hb * HEAD_DIM, (hb + 1) * HEAD_DIM)
            b_ref[:, cols] = _rope(acc[:, cols], cos_t, sin_t).astype(BF16)

    @pl.when(j == J_KV)
    def _():
        cos_t, sin_t = _rope_tables(freq_ref, rc_ref, rs_ref, i, 1.0)
        acc = project()
        for hb in range(N_KV_HEADS):
            cols = slice(hb * HEAD_DIM, (hb + 1) * HEAD_DIM)
            b_ref[:, cols] = _rope(acc[:, cols], cos_t, sin_t).astype(BF16)
        b_ref[:, KV_WIDTH:] = acc[:, KV_WIDTH:].astype(BF16)


def _inproj(x2, norm_w, w_in_bf16, rope_inputs, lru_params, w_gates):
    s = x2.shape[0]
    n_chunks = s // INPROJ_TM
    n_sub = s // LRU_SUB
    halo_per_chunk = INPROJ_TM // BF16_SUBLANES
    n_halo = s // BF16_SUBLANES
    lru_col = lambda i, j: (i, jnp.clip(j - J_SILU, 0, LRU_STEPS - 1))
    const2 = lambda i, j: (0, 0)
    const3 = lambda i, j: (0, 0, 0)
    scan_scratch = pltpu.VMEM((LRU_NK, INPROJ_TM, LANES), F32)
    return pl.pallas_call(
        functools.partial(_inproj_kernel, n_chunks),
        out_shape=(jax.ShapeDtypeStruct((s, B_WIDTH), BF16),
                   jax.ShapeDtypeStruct((s, D_MODEL), F32),
                   jax.ShapeDtypeStruct((s, D_MODEL), BF16),
                   jax.ShapeDtypeStruct((s, D_MODEL), BF16),
                   jax.ShapeDtypeStruct((n_sub, D_MODEL), F32),
                   jax.ShapeDtypeStruct((n_sub, D_MODEL), F32),
                   jax.ShapeDtypeStruct((n_sub, D_MODEL), F32),
                   jax.ShapeDtypeStruct((n_sub, D_MODEL), F32)),
        grid=(n_chunks, J_END),
        in_specs=[
            pl.BlockSpec(memory_space=pl.ANY),
            pl.BlockSpec((BF16_SUBLANES, D_MODEL),
                         lambda i, j: (jnp.minimum((i + 1) * halo_per_chunk, n_halo - 1), 0)),
            pl.BlockSpec((1, D_MODEL), const2),
            pl.BlockSpec((D_MODEL, INPROJ_TN), lambda i, j: (0, _weight_block(j))),
            pl.BlockSpec((1, LANES), const2),
            pl.BlockSpec((INPROJ_TM, LANES), const2, pipeline_mode=pl.Buffered(1)),
            pl.BlockSpec((INPROJ_TM, LANES), const2, pipeline_mode=pl.Buffered(1)),
            pl.BlockSpec((LRU_BLOCKS, P_ROWS, LANES), const3),
            pl.BlockSpec((LRU_BLOCKS, LRU_BLOCK_DIM, GATE_COLS), const3),
        ],
        out_specs=(
            pl.BlockSpec((INPROJ_TM, INPROJ_TN), lambda i, j: (i, jnp.maximum(j - J_SILU, 0))),
            pl.BlockSpec((INPROJ_TM, LRU_CW), lru_col),
            pl.BlockSpec((INPROJ_TM, LRU_CW), lru_col),
            pl.BlockSpec((INPROJ_TM, LRU_CW), lru_col),
            pl.BlockSpec((SUBLANES, LRU_CW), lru_col),
            pl.BlockSpec((SUBLANES, LRU_CW), lru_col),
            pl.BlockSpec((SUBLANES, LRU_CW), lru_col),
            pl.BlockSpec((SUBLANES, LRU_CW), lru_col),
        ),
        scratch_shapes=[
            pltpu.VMEM((INPROJ_TM, D_MODEL), F32),
            pltpu.SemaphoreType.DMA(()),
            pltpu.VMEM((XN_ROWS, D_MODEL), BF16),
            pltpu.VMEM((LRU_BLOCKS, U_ROWS, LANES), F32),
            pltpu.VMEM((LRU_BLOCKS, SUBLANES, LANES), F32),
            scan_scratch,
            pltpu.VMEM((LRU_NK, INPROJ_TM, LANES), BF16),
            scan_scratch, scan_scratch, scan_scratch, scan_scratch,
        ],
        compiler_params=pltpu.CompilerParams(
            dimension_semantics=("arbitrary", "arbitrary"),
            vmem_limit_bytes=VMEM_LIMIT),
        name="inproj",
    )(x2, x2, norm_w, w_in_bf16, *rope_inputs, lru_params, w_gates)


def _lru_carry_kernel(ef_ref, pef_ref, eb_ref, peb_ref, cf_ref, cbk_ref):
    n_sub = ef_ref.shape[0]

    def fwd(g, carry):
        row = pl.ds(g, 1)
        cf_ref[row, :] = carry
        return ef_ref[row, :] + pef_ref[row, :] * carry

    def bwd(i, carry):
        row = pl.ds(n_sub - 1 - i, 1)
        cbk_ref[row, :] = carry
        return eb_ref[row, :] + peb_ref[row, :] * carry

    zero = jnp.zeros((1, ef_ref.shape[1]), F32)
    lax.fori_loop(0, n_sub, fwd, zero)
    lax.fori_loop(0, n_sub, bwd, zero)


def _lru_carry(ef, pef, eb, peb):
    shape = jax.ShapeDtypeStruct(ef.shape, F32)
    return pl.pallas_call(
        _lru_carry_kernel,
        out_shape=(shape, shape),
        name="lru_carry",
    )(ef, pef, eb, peb)


ATT_QB = 2
ATT_TQ = ATT_QB * BLOCK


def _attn_kernel(n_steps, sink_ref, q_ref, kl_ref, kc_ref, kr_ref, vl_ref, vc_ref, vr_ref,
                 sg_ref, o_ref):
    step = pl.program_id(0)
    k_ext = jnp.concatenate([kl_ref[...], kc_ref[...], kr_ref[...]], axis=0)
    v_ext = jnp.concatenate([vl_ref[...], vc_ref[...], vr_ref[...]], axis=0)

    q_idx = lax.broadcasted_iota(jnp.int32, (BLOCK, 3 * BLOCK), 0)
    s_idx = lax.broadcasted_iota(jnp.int32, (BLOCK, 3 * BLOCK), 1)
    rel = s_idx - BLOCK - q_idx
    band_ok = (rel <= WINDOW) & (rel >= -WINDOW)
    head_of_row = lax.broadcasted_iota(jnp.int32, (GROUP * BLOCK, 1), 0) >> 7

    for qb in range(ATT_QB):
        ok = band_ok
        if qb == 0:
            ok = ok & ((s_idx >= BLOCK) | (step > 0))
        if qb == ATT_QB - 1:
            ok = ok & ((s_idx < 2 * BLOCK) | (step < n_steps - 1))
        bias = jnp.where(ok, 0.0, MASK_VALUE)
        bias = jnp.concatenate([bias] * GROUP, axis=0)
        q_rows = slice(qb * BLOCK, (qb + 1) * BLOCK)
        k_rows = slice(qb * BLOCK, (qb + 3) * BLOCK)
        for kh in range(N_KV_HEADS):
            kv_cols = slice(kh * HEAD_DIM, (kh + 1) * HEAD_DIM)
            q = jnp.concatenate(
                [q_ref[q_rows, (kh * GROUP + g) * HEAD_DIM:(kh * GROUP + g + 1) * HEAD_DIM]
                 for g in range(GROUP)], axis=0)
            scores = lax.dot_general(q, k_ext[k_rows, kv_cols], (((1,), (1,)), ((), ())),
                                     preferred_element_type=F32)
            scores = jnp.where(bias == 0.0, scores, MASK_VALUE)
            sink = jnp.zeros((GROUP * BLOCK, 1), F32)
            for g in range(GROUP):
                sink = jnp.where(head_of_row == g, sink_ref[kh * GROUP + g] * LOG2E, sink)
            m = jnp.maximum(jnp.max(scores, axis=-1, keepdims=True), sink)
            e = jnp.exp2(scores - m)
            denom = jnp.sum(e, axis=-1, keepdims=True) + jnp.exp2(sink - m)
            out = jnp.dot(e.astype(BF16), v_ext[k_rows, kv_cols],
                          preferred_element_type=F32) / denom
            for g in range(GROUP):
                cols = slice((kh * GROUP + g) * HEAD_DIM, (kh * GROUP + g + 1) * HEAD_DIM)
                gate = sg_ref[q_rows, cols].astype(F32)
                o_ref[q_rows, cols] = (out[g * BLOCK:(g + 1) * BLOCK, :] * gate).astype(o_ref.dtype)


def _attention(zb, sink):
    s = zb.shape[0]
    n_steps = s // ATT_TQ
    n_blocks = s // BLOCK
    aw = N_HEADS * HEAD_DIM

    def left(t):
        return jnp.maximum(t * ATT_QB - 1, 0)

    def right(t):
        return jnp.minimum((t + 1) * ATT_QB, n_blocks - 1)

    kcol = B_K // KV_WIDTH
    vcol = B_V // KV_WIDTH
    in_specs = [
        pl.BlockSpec(memory_space=pltpu.SMEM),
        pl.BlockSpec((ATT_TQ, aw), lambda t: (t, B_Q // aw)),
        pl.BlockSpec((BLOCK, KV_WIDTH), lambda t: (left(t), kcol)),
        pl.BlockSpec((ATT_TQ, KV_WIDTH), lambda t: (t, kcol)),
        pl.BlockSpec((BLOCK, KV_WIDTH), lambda t: (right(t), kcol)),
        pl.BlockSpec((BLOCK, KV_WIDTH), lambda t: (left(t), vcol)),
        pl.BlockSpec((ATT_TQ, KV_WIDTH), lambda t: (t, vcol)),
        pl.BlockSpec((BLOCK, KV_WIDTH), lambda t: (right(t), vcol)),
        pl.BlockSpec((ATT_TQ, aw), lambda t: (t, B_SG_ATT // aw)),
    ]
    return pl.pallas_call(
        functools.partial(_attn_kernel, n_steps),
        out_shape=jax.ShapeDtypeStruct((s, aw), BF16),
        grid=(n_steps,),
        in_specs=in_specs,
        out_specs=pl.BlockSpec((ATT_TQ, aw), lambda t: (t, 0)),
        compiler_params=pltpu.CompilerParams(
            dimension_semantics=("parallel",),
            vmem_limit_bytes=VMEM_LIMIT),
        name="attn",
    )(sink, zb, zb, zb, zb, zb, zb, zb, zb)


MERGE_TM = 256


def _merge_kernel(h_ref, pf_ref, pb_ref, sg_ref, yb_ref, ga_ref, gb_ref, cf_ref, cbk_ref,
                  wa_ref, wb_ref, o_ref):
    i = pl.program_id(0)
    parts = []
    for q in range(MERGE_TM // LRU_SUB):
        g = i * (MERGE_TM // LRU_SUB) + q
        rows = slice(q * LRU_SUB, (q + 1) * LRU_SUB)
        h = (h_ref[rows, :]
             + pf_ref[rows, :].astype(F32) * cf_ref[pl.ds(g, 1), :]
             + pb_ref[rows, :].astype(F32) * cbk_ref[pl.ds(g, 1), :])
        parts.append((h * sg_ref[rows, :].astype(F32)).astype(BF16))
    y_a = jnp.concatenate(parts, axis=0)
    pa = jnp.dot(y_a, wa_ref[...], preferred_element_type=F32)
    pb = jnp.dot(yb_ref[...], wb_ref[...], preferred_element_type=F32)
    o_ref[...] = (ga_ref[...].astype(F32) * pa + gb_ref[...].astype(F32) * pb).astype(BF16)


def _merge(h_loc, p_f, p_b, y_b, zb, c_f, c_b, wa, wb):
    s = h_loc.shape[0]
    row = lambda i: (i, 0)
    const = lambda i: (0, 0)
    resident = functools.partial(pl.BlockSpec, pipeline_mode=pl.Buffered(1))
    tile = (MERGE_TM, D_MODEL)
    return pl.pallas_call(
        _merge_kernel,
        out_shape=jax.ShapeDtypeStruct((s, D_MODEL), BF16),
        grid=(s // MERGE_TM,),
        in_specs=[
            pl.BlockSpec(tile, row),
            pl.BlockSpec(tile, row),
            pl.BlockSpec(tile, row),
            pl.BlockSpec(tile, lambda i: (i, B_SG_LRU // D_MODEL)),
            pl.BlockSpec(tile, row),
            pl.BlockSpec(tile, lambda i: (i, B_GM_LRU // D_MODEL)),
            pl.BlockSpec(tile, lambda i: (i, B_GM_ATT // D_MODEL)),
            resident(c_f.shape, const),
            resident(c_b.shape, const),
            resident((D_MODEL, D_MODEL), const),
            resident((D_MODEL, D_MODEL), const),
        ],
        out_specs=pl.BlockSpec(tile, row),
        compiler_params=pltpu.CompilerParams(
            dimension_semantics=("parallel",),
            vmem_limit_bytes=VMEM_LIMIT),
        name="merge",
    )(h_loc, p_f, p_b, zb, y_b, zb, zb, c_f, c_b, wa, wb)


OUT_TM = 512


def _out_kernel(m_ref, x_ref, wo_ref, nw_ref, o_ref):
    y = jnp.dot(m_ref[...], wo_ref[...], preferred_element_type=F32)
    ms = jnp.mean(y * y, axis=-1, keepdims=True)
    o_ref[...] = x_ref[...] + y * lax.rsqrt(ms + NORM_EPS) * nw_ref[...]


def _out_proj(merged, x2, wo, norm_w):
    s = x2.shape[0]
    row = lambda i: (i, 0)
    const = lambda i: (0, 0)
    return pl.pallas_call(
        _out_kernel,
        out_shape=jax.ShapeDtypeStruct((s, D_MODEL), F32),
        grid=(s // OUT_TM,),
        in_specs=[
            pl.BlockSpec((OUT_TM, D_MODEL), row),
            pl.BlockSpec((OUT_TM, D_MODEL), row),
            pl.BlockSpec((D_MODEL, D_MODEL), const, pipeline_mode=pl.Buffered(1)),
            pl.BlockSpec((1, D_MODEL), const),
        ],
        out_specs=pl.BlockSpec((OUT_TM, D_MODEL), row),
        compiler_params=pltpu.CompilerParams(
            dimension_semantics=("parallel",),
            vmem_limit_bytes=VMEM_LIMIT),
        name="out_proj",
    )(merged, x2, wo, norm_w)


def _rope_inputs():
    inv_freq = ROPE_THETA ** (-jnp.arange(0, ROT_DIM, 2, dtype=F32) / ROT_DIM)
    freq = jnp.concatenate([inv_freq, inv_freq, jnp.zeros((LANES - ROT_DIM,), F32)])[None, :]
    ang = jnp.arange(INPROJ_TM, dtype=F32)[:, None] * freq
    return freq, jnp.cos(ang), jnp.sin(ang)


def _gate_weights(w_r, w_i):
    w = jnp.concatenate([w_r[0], w_i[0], w_r[1], w_i[1]], axis=-1)
    return (0.5 * w).astype(BF16)


def _lru_params(conv_w, conv_b, b_r, b_i, lam):
    rows = jnp.concatenate([
        conv_w, conv_b[None, :],
        0.5 * jnp.stack([b_r[0], b_i[0], b_r[1], b_i[1]]),
        lam,
        jnp.zeros((P_ROWS - P_LAMBDA - 2, D_MODEL), F32)], axis=0)
    return rows.reshape(P_ROWS, LRU_BLOCKS, LANES).transpose(1, 0, 2)


def kernel(x, norm_pre_w, w_in, conv_w, conv_b, lru_w_r, lru_b_r, lru_w_i, lru_b_i,
           lru_lambda, attn_sink, w_proj_a, w_proj_b, w_out, norm_post_w):
    bsz, s, d = x.shape
    depth = w_in.shape[0]
    rope_inputs = _rope_inputs()
    outs = []
    for bi in range(bsz):
        xb = x[bi]
        for l in range(depth):
            zb, h_loc, p_f, p_b, ef, pef, eb, peb = _inproj(
                xb, norm_pre_w[l][None, :], w_in[l].astype(BF16), rope_inputs,
                _lru_params(conv_w[l], conv_b[l], lru_b_r[l], lru_b_i[l], lru_lambda[l]),
                _gate_weights(lru_w_r[l], lru_w_i[l]))
            c_f, c_b = _lru_carry(ef, pef, eb, peb)
            y_b = _attention(zb, attn_sink[l])
            merged = _merge(h_loc, p_f, p_b, y_b, zb, c_f, c_b,
                            w_proj_a[l].astype(BF16), w_proj_b[l].astype(BF16))
            xb = _out_proj(merged, xb, w_out[l].astype(BF16), norm_post_w[l][None, :])
        outs.append(xb)
    return jnp.stack(outs, axis=0)
```

```python
import functools
import math

import jax
import jax.numpy as jnp
from jax import lax
from jax.experimental import pallas as pl
from jax.experimental.pallas import tpu as pltpu

F32 = jnp.float32
BF16 = jnp.bfloat16

D_MODEL = 2048
LRU_BLOCK_DIM = 128
LRU_BLOCKS = D_MODEL // LRU_BLOCK_DIM
LRU_C = 8.0
N_HEADS = 16
N_KV_HEADS = 4
GROUP = N_HEADS // N_KV_HEADS
HEAD_DIM = 128
KV_WIDTH = N_KV_HEADS * HEAD_DIM
BLOCK = 128
WINDOW = 128
ROPE_THETA = 500000.0
ROT_DIM = HEAD_DIM // 4
NORM_EPS = 1e-6
MASK_VALUE = -1e30
LOG2E = math.log2(math.e)

B_SG_LRU, B_SG_ATT, B_GM_LRU, B_GM_ATT, B_Q, B_K, B_V = (
    0, 2048, 4096, 6144, 8192, 10240, 10752)
B_WIDTH = 11264

SUBLANES = 8
BF16_SUBLANES = 16
LANES = 128
VMEM_LIMIT = 56 * 1024 * 1024


def _sigmoid(x):
    return 0.5 * jnp.tanh(0.5 * x) + 0.5


def _silu(x):
    hx = 0.5 * x
    return hx * jnp.tanh(hx) + hx


def _sqrt_nonneg(x):
    return jnp.where(x > 0.0, x * lax.rsqrt(x), 0.0)


INPROJ_TM = 1024
INPROJ_TN = 1024
ROPE_ROWS = 256
NORM_ROWS = 64
XN_ROWS = INPROJ_TM + BF16_SUBLANES
J_U, J_SILU, J_SIG, J_Q, J_KV, J_END = 0, 2, 6, 10, 12, 13

LRU_SUB = 128
LRU_CW = 256
LRU_NK = LRU_CW // LRU_BLOCK_DIM
LRU_STEPS = D_MODEL // LRU_CW
SLABS_PER_J = INPROJ_TN // LANES
CONV_LEFT = 2
U_ROWS = (LRU_SUB + 3) * SUBLANES
CONV_TILE = 128
GATE_COLS = 4 * LRU_BLOCK_DIM
P_CONV_W, P_CONV_B, P_BIAS, P_LAMBDA, P_ROWS = 0, 4, 5, 9, 16


def _weight_block(j):
    return jnp.where(j < 4, j, jnp.where(j < 10, j + 3, j - 6))


def _rope_tables(freq_ref, rc_ref, rs_ref, chunk, scale):
    lane = lax.broadcasted_iota(jnp.int32, (1, LANES), 1)
    base = (chunk * INPROJ_TM).astype(F32) * freq_ref[...]
    cb, sb = jnp.cos(base) * scale, jnp.sin(base) * scale
    sign = jnp.where(lane < ROT_DIM // 2, -1.0, jnp.where(lane < ROT_DIM, 1.0, 0.0))
    rc, rs = rc_ref[...], rs_ref[...]
    cos_t = cb * rc - sb * rs
    sin_t = (sign * sb) * rc + (sign * cb) * rs
    return cos_t, sin_t


def _rope(t, cos_t, sin_t):
    half = ROT_DIM // 2
    lane = lax.broadcasted_iota(jnp.int32, t.shape, 1)
    partner = jnp.where(lane < half,
                        pltpu.roll(t, HEAD_DIM - half, axis=1),
                        pltpu.roll(t, half, axis=1))
    return t * cos_t + partner * sin_t


def _lru_conv(kp, lp_ref, u_s, uc_s, ucb_s):
    for k in range(LRU_NK):
        kk = kp * LRU_NK + k
        par = lp_ref.at[kk]
        u_k = u_s.at[kk]
        for t0 in range(0, INPROJ_TM, CONV_TILE):
            y = par[P_CONV_B:P_CONV_B + 1, :]
            for tap in range(4):
                lo = t0 + tap * SUBLANES
                y = y + u_k[lo:lo + CONV_TILE, :] * par[P_CONV_W + tap:P_CONV_W + tap + 1, :]
            uc_s[k, t0:t0 + CONV_TILE, :] = y
            ucb_s[k, t0:t0 + CONV_TILE, :] = y.astype(BF16)


def _lru_gates(kp, wg_ref, ucb_s, z_s):
    zrf_s, zif_s, zrb_s, zib_s = z_s
    for k in range(LRU_NK):
        z = jnp.dot(ucb_s[k], wg_ref[kp * LRU_NK + k], preferred_element_type=F32)
        zrf_s[k] = z[:, 0 * LANES:1 * LANES]
        zif_s[k] = z[:, 1 * LANES:2 * LANES]
        zrb_s[k] = z[:, 2 * LANES:3 * LANES]
        zib_s[k] = z[:, 3 * LANES:4 * LANES]


def _lru_scan(kp, lp_ref, uc_s, z_s, h_ref, pf_ref, pb_ref, ef_ref, pef_ref, eb_ref, peb_ref):
    zrf_s, zif_s, zrb_s, zib_s = z_s
    consts = []
    for k in range(LRU_NK):
        par = lp_ref.at[kp * LRU_NK + k]
        per_dir = []
        for d in range(2):
            nl = -par[P_LAMBDA + d:P_LAMBDA + d + 1, :]
            softplus = jnp.maximum(nl, 0.0) + jnp.log1p(jnp.exp(-jnp.abs(nl)))
            per_dir.append((
                jnp.broadcast_to(par[P_BIAS + 2 * d:P_BIAS + 2 * d + 1, :], (SUBLANES, LANES)),
                jnp.broadcast_to(par[P_BIAS + 2 * d + 1:P_BIAS + 2 * d + 2, :], (SUBLANES, LANES)),
                jnp.broadcast_to(-0.25 * LRU_C * softplus, (SUBLANES, LANES))))
        consts.append(per_dir)

    half = LRU_SUB // 2

    def step(k, d, r, h, p):
        rows = slice(r * SUBLANES, (r + 1) * SUBLANES)
        zr_s, zi_s = (zrf_s, zif_s) if d == 0 else (zrb_s, zib_s)
        br_h, bi_h, dq = consts[k][d]
        tr = jnp.tanh(zr_s[k, rows, :] + br_h)
        th = jnp.tanh(tr * dq + dq)
        inv = 1.0 / (1.0 - th)
        a = (1.0 + th) * inv
        root = _sqrt_nonneg(-th)
        ti = jnp.tanh(zi_s[k, rows, :] + bi_h)
        b = (root * inv) * ((ti + 1.0) * uc_s[k, rows, :])
        h = a * h + b
        p = a * p
        crossed = (r >= half) if d == 0 else (r < half)
        if crossed:
            other = zrb_s if d == 0 else zrf_s
            zrf_s[k, rows, :] = h + other[k, rows, :]
        else:
            zr_s[k, rows, :] = h
        zi_s[k, rows, :] = p
        return h, p

    zero = jnp.zeros((SUBLANES, LANES), F32)
    one = jnp.ones((SUBLANES, LANES), F32)
    state = [[zero, one, zero, one] for _ in range(LRU_NK)]
    for it in range(LRU_SUB):
        for k in range(LRU_NK):
            hf, pf, hb, pb = state[k]
            hf, pf = step(k, 0, it, hf, pf)
            hb, pb = step(k, 1, LRU_SUB - 1 - it, hb, pb)
            state[k] = [hf, pf, hb, pb]

    for k in range(LRU_NK):
        cols = slice(k * LANES, (k + 1) * LANES)
        hf, pf, hb, pb = state[k]
        ef_ref[:, cols] = hf
        pef_ref[:, cols] = pf
        eb_ref[:, cols] = hb
        peb_ref[:, cols] = pb
        for s in range(SUBLANES):
            rows = slice(s * LRU_SUB, (s + 1) * LRU_SUB)
            sel = pl.ds(s, LRU_SUB, stride=SUBLANES)
            h_ref[rows, cols] = zrf_s[k, sel, :]
            pf_ref[rows, cols] = zif_s[k, sel, :].astype(BF16)
            pb_ref[rows, cols] = zib_s[k, sel, :].astype(BF16)


def _inproj_kernel(n_chunks, x_ref, xh_ref, nw_ref, w_ref, freq_ref, rc_ref, rs_ref, lp_ref, wg_ref,
                   b_ref, h_ref, pf_ref, pb_ref, ef_ref, pef_ref, eb_ref, peb_ref,
                   x_buf, x_sem, xn_ref, u_s, uprev_s, uc_s, ucb_s, zrf_s, zif_s, zrb_s, zib_s):
    i = pl.program_id(0)
    j = pl.program_id(1)
    z_s = (zrf_s, zif_s, zrb_s, zib_s)
    lru_outs = (h_ref, pf_ref, pb_ref, ef_ref, pef_ref, eb_ref, peb_ref)

    def x_copy(chunk):
        rows = pl.ds(pl.multiple_of(chunk * INPROJ_TM, INPROJ_TM), INPROJ_TM)
        return pltpu.make_async_copy(x_ref.at[rows, :], x_buf, x_sem)

    @pl.when((i == 0) & (j == 0))
    def _():
        x_copy(0).start()

    @pl.when((j == 1) & (i < n_chunks - 1))
    def _():
        x_copy(i + 1).start()

    def project(lo=0, hi=INPROJ_TM):
        return jnp.dot(xn_ref[lo:hi, :], w_ref[...], preferred_element_type=F32)

    def project_with_lru(epilogue):
        kp = j - J_SILU
        _lru_gates(kp, wg_ref, ucb_s, z_s)
        b_ref[...] = epilogue(project()).astype(BF16)
        _lru_scan(kp, lp_ref, uc_s, z_s, *lru_outs)
        _lru_conv(jnp.minimum(kp + 1, LRU_STEPS - 1), lp_ref, u_s, uc_s, ucb_s)

    def normalise(xv):
        ms = jnp.mean(xv * xv, axis=-1, keepdims=True)
        return (xv * lax.rsqrt(ms + NORM_EPS) * nw_ref[...]).astype(BF16)

    @pl.when(j == 0)
    def _():
        x_copy(i).wait()

        def body(r, carry):
            rows = pl.ds(pl.multiple_of(r * NORM_ROWS, NORM_ROWS), NORM_ROWS)
            xn_ref[rows, :] = normalise(x_buf[rows, :])
            return carry
        lax.fori_loop(0, INPROJ_TM // NORM_ROWS, body, 0)
        xn_ref[INPROJ_TM:, :] = normalise(xh_ref[...])

    @pl.when(j < J_SILU)
    def _():
        acc = project(0, XN_ROWS)
        sub = lax.broadcasted_iota(jnp.int32, (SUBLANES, LANES), 0)
        first_slab = CONV_LEFT * SUBLANES
        for k in range(SLABS_PER_J):
            kk = j * SLABS_PER_J + k
            cols = slice(k * LANES, (k + 1) * LANES)
            u_k = u_s.at[kk]
            for s in range(SUBLANES):
                u_k[pl.ds(first_slab + s, LRU_SUB, stride=SUBLANES), :] = (
                    acc[s * LRU_SUB:(s + 1) * LRU_SUB, cols])
            prev = uprev_s[kk]
            prev1 = jnp.where(i > 0, prev[SUBLANES - 1:SUBLANES, :], 0.0)
            prev2 = jnp.where(i > 0, prev[SUBLANES - 2:SUBLANES - 1, :], 0.0)
            nxt = jnp.where(i < n_chunks - 1, acc[INPROJ_TM:INPROJ_TM + 1, cols], 0.0)
            last1 = u_k[pl.ds(first_slab + (LRU_SUB - 1) * SUBLANES, SUBLANES), :]
            last2 = u_k[pl.ds(first_slab + (LRU_SUB - 2) * SUBLANES, SUBLANES), :]
            head = u_k[pl.ds(first_slab, SUBLANES), :]
            u_k[pl.ds(SUBLANES, SUBLANES), :] = jnp.where(sub == 0, prev1, pltpu.roll(last1, 1, axis=0))
            u_k[pl.ds(0, SUBLANES), :] = jnp.where(sub == 0, prev2, pltpu.roll(last2, 1, axis=0))
            u_k[pl.ds(first_slab + LRU_SUB * SUBLANES, SUBLANES), :] = jnp.where(
                sub == SUBLANES - 1, nxt, pltpu.roll(head, SUBLANES - 1, axis=0))
            uprev_s[kk] = acc[INPROJ_TM - SUBLANES:INPROJ_TM, cols]
        _lru_conv(0, lp_ref, u_s, uc_s, ucb_s)

    @pl.when((j >= J_SILU) & (j < J_SIG))
    def _():
        project_with_lru(_silu)

    @pl.when((j >= J_SIG) & (j < J_Q))
    def _():
        project_with_lru(_sigmoid)

    @pl.when((j >= J_Q) & (j < J_KV))
    def _():
        cos_t, sin_t = _rope_tables(freq_ref, rc_ref, rs_ref, i, HEAD_DIM ** -0.5 * LOG2E)
        for lo in range(0, INPROJ_TM, ROPE_ROWS):
            rows = slice(lo, lo + ROPE_ROWS)
            acc = project(lo, lo + ROPE_ROWS)
            for hb in range(INPROJ_TN // HEAD_DIM):
                cols = slice(hb * HEAD_DIM, (hb + 1) * HEAD_DIM)
                b_ref[rows, cols] = _rope(acc[:, cols], cos_t[rows], sin_t[rows]).astype(BF16)

    @pl.when(j == J_KV)
    def _():
        cos_t, sin_t = _rope_tables(freq_ref, rc_ref, rs_ref, i, 1.0)
        for lo in range(0, INPROJ_TM, ROPE_ROWS):
            rows = slice(lo, lo + ROPE_ROWS)
            acc = project(lo, lo + ROPE_ROWS)
            for hb in range(N_KV_HEADS):
                cols = slice(hb * HEAD_DIM, (hb + 1) * HEAD_DIM)
                b_ref[rows, cols] = _rope(acc[:, cols], cos_t[rows], sin_t[rows]).astype(BF16)
            b_ref[rows, KV_WIDTH:] = acc[:, KV_WIDTH:].astype(BF16)


def _inproj(x2, norm_w, w_in_bf16, rope_inputs, lru_params, w_gates):
    s = x2.shape[0]
    n_chunks = s // INPROJ_TM
    n_sub = s // LRU_SUB
    halo_per_chunk = INPROJ_TM // BF16_SUBLANES
    n_halo = s // BF16_SUBLANES
    lru_col = lambda i, j: (i, jnp.clip(j - J_SILU, 0, LRU_STEPS - 1))
    const2 = lambda i, j: (0, 0)
    const3 = lambda i, j: (0, 0, 0)
    scan_scratch = pltpu.VMEM((LRU_NK, INPROJ_TM, LANES), F32)
    return pl.pallas_call(
        functools.partial(_inproj_kernel, n_chunks),
        out_shape=(jax.ShapeDtypeStruct((s, B_WIDTH), BF16),
                   jax.ShapeDtypeStruct((s, D_MODEL), F32),
                   jax.ShapeDtypeStruct((s, D_MODEL), BF16),
                   jax.ShapeDtypeStruct((s, D_MODEL), BF16),
                   jax.ShapeDtypeStruct((n_sub, D_MODEL), F32),
                   jax.ShapeDtypeStruct((n_sub, D_MODEL), F32),
                   jax.ShapeDtypeStruct((n_sub, D_MODEL), F32),
                   jax.ShapeDtypeStruct((n_sub, D_MODEL), F32)),
        grid=(n_chunks, J_END),
        in_specs=[
            pl.BlockSpec(memory_space=pl.ANY),
            pl.BlockSpec((BF16_SUBLANES, D_MODEL),
                         lambda i, j: (jnp.minimum((i + 1) * halo_per_chunk, n_halo - 1), 0)),
            pl.BlockSpec((1, D_MODEL), const2),
            pl.BlockSpec((D_MODEL, INPROJ_TN), lambda i, j: (0, _weight_block(j))),
            pl.BlockSpec((1, LANES), const2),
            pl.BlockSpec((INPROJ_TM, LANES), const2, pipeline_mode=pl.Buffered(1)),
            pl.BlockSpec((INPROJ_TM, LANES), const2, pipeline_mode=pl.Buffered(1)),
            pl.BlockSpec((LRU_BLOCKS, P_ROWS, LANES), const3),
            pl.BlockSpec((LRU_BLOCKS, LRU_BLOCK_DIM, GATE_COLS), const3),
        ],
        out_specs=(
            pl.BlockSpec((INPROJ_TM, INPROJ_TN), lambda i, j: (i, jnp.maximum(j - J_SILU, 0))),
            pl.BlockSpec((INPROJ_TM, LRU_CW), lru_col),
            pl.BlockSpec((INPROJ_TM, LRU_CW), lru_col),
            pl.BlockSpec((INPROJ_TM, LRU_CW), lru_col),
            pl.BlockSpec((SUBLANES, LRU_CW), lru_col),
            pl.BlockSpec((SUBLANES, LRU_CW), lru_col),
            pl.BlockSpec((SUBLANES, LRU_CW), lru_col),
            pl.BlockSpec((SUBLANES, LRU_CW), lru_col),
        ),
        scratch_shapes=[
            pltpu.VMEM((INPROJ_TM, D_MODEL), F32),
            pltpu.SemaphoreType.DMA(()),
            pltpu.VMEM((XN_ROWS, D_MODEL), BF16),
            pltpu.VMEM((LRU_BLOCKS, U_ROWS, LANES), F32),
            pltpu.VMEM((LRU_BLOCKS, SUBLANES, LANES), F32),
            scan_scratch,
            pltpu.VMEM((LRU_NK, INPROJ_TM, LANES), BF16),
            scan_scratch, scan_scratch, scan_scratch, scan_scratch,
        ],
        compiler_params=pltpu.CompilerParams(
            dimension_semantics=("arbitrary", "arbitrary"),
            vmem_limit_bytes=VMEM_LIMIT),
        name="inproj",
    )(x2, x2, norm_w, w_in_bf16, *rope_inputs, lru_params, w_gates)


def _lru_carry_kernel(ef_ref, pef_ref, eb_ref, peb_ref, cf_ref, cbk_ref):
    n_sub = ef_ref.shape[0]

    def fwd(g, carry):
        row = pl.ds(g, 1)
        cf_ref[row, :] = carry
        return ef_ref[row, :] + pef_ref[row, :] * carry

    def bwd(i, carry):
        row = pl.ds(n_sub - 1 - i, 1)
        cbk_ref[row, :] = carry
        return eb_ref[row, :] + peb_ref[row, :] * carry

    zero = jnp.zeros((1, ef_ref.shape[1]), F32)
    lax.fori_loop(0, n_sub, fwd, zero)
    lax.fori_loop(0, n_sub, bwd, zero)


def _lru_carry(ef, pef, eb, peb):
    shape = jax.ShapeDtypeStruct(ef.shape, F32)
    return pl.pallas_call(
        _lru_carry_kernel,
        out_shape=(shape, shape),
        name="lru_carry",
    )(ef, pef, eb, peb)


ATT_QB = 2
ATT_TQ = ATT_QB * BLOCK


def _attn_kernel(n_steps, sink_ref, q_ref, kl_ref, kc_ref, kr_ref, vl_ref, vc_ref, vr_ref,
                 sg_ref, o_ref):
    step = pl.program_id(0)
    k_ext = jnp.concatenate([kl_ref[...], kc_ref[...], kr_ref[...]], axis=0)
    v_ext = jnp.concatenate([vl_ref[...], vc_ref[...], vr_ref[...]], axis=0)

    q_idx = lax.broadcasted_iota(jnp.int32, (BLOCK, BLOCK), 0)
    s_idx = lax.broadcasted_iota(jnp.int32, (BLOCK, BLOCK), 1)
    tri_left = jnp.where(s_idx >= q_idx, 0.0, MASK_VALUE)
    tri_right = jnp.where(s_idx <= q_idx, 0.0, MASK_VALUE)

    for kh in range(N_KV_HEADS):
        kv_cols = slice(kh * HEAD_DIM, (kh + 1) * HEAD_DIM)
        sinks = [sink_ref[kh * GROUP + g] * LOG2E for g in range(GROUP)]
        for qb in range(ATT_QB):
            bias_l, bias_r = tri_left, tri_right
            if qb == 0:
                bias_l = jnp.where(step > 0, tri_left, MASK_VALUE)
            if qb == ATT_QB - 1:
                bias_r = jnp.where(step < n_steps - 1, tri_right, MASK_VALUE)
            q_rows = slice(qb * BLOCK, (qb + 1) * BLOCK)
            k_rows = slice(qb * BLOCK, (qb + 3) * BLOCK)
            q = jnp.concatenate(
                [q_ref[q_rows, (kh * GROUP + g) * HEAD_DIM:(kh * GROUP + g + 1) * HEAD_DIM]
                 for g in range(GROUP)], axis=0)
            scores = lax.dot_general(q, k_ext[k_rows, kv_cols], (((1,), (1,)), ((), ())),
                                     preferred_element_type=F32)
            probs, denoms = [], []
            for g in range(GROUP):
                head = scores[g * BLOCK:(g + 1) * BLOCK, :]
                s_l = head[:, :BLOCK] + bias_l
                s_c = head[:, BLOCK:2 * BLOCK]
                s_r = head[:, 2 * BLOCK:] + bias_r
                m = jnp.max(jnp.maximum(jnp.maximum(s_l, s_c), s_r), axis=-1, keepdims=True)
                m = jnp.maximum(m, sinks[g])
                e_l, e_c, e_r = jnp.exp2(s_l - m), jnp.exp2(s_c - m), jnp.exp2(s_r - m)
                denoms.append(jnp.sum(e_l + e_c + e_r, axis=-1, keepdims=True) + jnp.exp2(sinks[g] - m))
                probs.append(jnp.concatenate(
                    [e_l.astype(BF16), e_c.astype(BF16), e_r.astype(BF16)], axis=1))
            out = jnp.dot(jnp.concatenate(probs, axis=0), v_ext[k_rows, kv_cols],
                          preferred_element_type=F32)
            for g in range(GROUP):
                cols = slice((kh * GROUP + g) * HEAD_DIM, (kh * GROUP + g + 1) * HEAD_DIM)
                gate = sg_ref[q_rows, cols].astype(F32)
                o_ref[q_rows, cols] = (out[g * BLOCK:(g + 1) * BLOCK, :] / denoms[g] * gate
                                       ).astype(o_ref.dtype)


def _attention(zb, sink):
    s = zb.shape[0]
    n_steps = s // ATT_TQ
    n_blocks = s // BLOCK
    aw = N_HEADS * HEAD_DIM

    def left(t):
        return jnp.maximum(t * ATT_QB - 1, 0)

    def right(t):
        return jnp.minimum((t + 1) * ATT_QB, n_blocks - 1)

    kcol = B_K // KV_WIDTH
    vcol = B_V // KV_WIDTH
    in_specs = [
        pl.BlockSpec(memory_space=pltpu.SMEM),
        pl.BlockSpec((ATT_TQ, aw), lambda t: (t, B_Q // aw)),
        pl.BlockSpec((BLOCK, KV_WIDTH), lambda t: (left(t), kcol)),
        pl.BlockSpec((ATT_TQ, KV_WIDTH), lambda t: (t, kcol)),
        pl.BlockSpec((BLOCK, KV_WIDTH), lambda t: (right(t), kcol)),
        pl.BlockSpec((BLOCK, KV_WIDTH), lambda t: (left(t), vcol)),
        pl.BlockSpec((ATT_TQ, KV_WIDTH), lambda t: (t, vcol)),
        pl.BlockSpec((BLOCK, KV_WIDTH), lambda t: (right(t), vcol)),
        pl.BlockSpec((ATT_TQ, aw), lambda t: (t, B_SG_ATT // aw)),
    ]
    return pl.pallas_call(
        functools.partial(_attn_kernel, n_steps),
        out_shape=jax.ShapeDtypeStruct((s, aw), BF16),
        grid=(n_steps,),
        in_specs=in_specs,
        out_specs=pl.BlockSpec((ATT_TQ, aw), lambda t: (t, 0)),
        compiler_params=pltpu.CompilerParams(
            dimension_semantics=("parallel",),
            vmem_limit_bytes=VMEM_LIMIT),
        name="attn",
    )(sink, zb, zb, zb, zb, zb, zb, zb, zb)


MERGE_TM = 256


def _merge_kernel(h_ref, pf_ref, pb_ref, sg_ref, yb_ref, ga_ref, gb_ref, cf_ref, cbk_ref,
                  wa_ref, wb_ref, o_ref):
    i = pl.program_id(0)
    parts = []
    for q in range(MERGE_TM // LRU_SUB):
        g = i * (MERGE_TM // LRU_SUB) + q
        rows = slice(q * LRU_SUB, (q + 1) * LRU_SUB)
        h = (h_ref[rows, :]
             + pf_ref[rows, :].astype(F32) * cf_ref[pl.ds(g, 1), :]
             + pb_ref[rows, :].astype(F32) * cbk_ref[pl.ds(g, 1), :])
        parts.append((h * sg_ref[rows, :].astype(F32)).astype(BF16))
    y_a = jnp.concatenate(parts, axis=0)
    pa = jnp.dot(y_a, wa_ref[...], preferred_element_type=F32)
    pb = jnp.dot(yb_ref[...], wb_ref[...], preferred_element_type=F32)
    o_ref[...] = (ga_ref[...].astype(F32) * pa + gb_ref[...].astype(F32) * pb).astype(BF16)


def _merge(h_loc, p_f, p_b, y_b, zb, c_f, c_b, wa, wb):
    s = h_loc.shape[0]
    row = lambda i: (i, 0)
    const = lambda i: (0, 0)
    resident = functools.partial(pl.BlockSpec, pipeline_mode=pl.Buffered(1))
    tile = (MERGE_TM, D_MODEL)
    return pl.pallas_call(
        _merge_kernel,
        out_shape=jax.ShapeDtypeStruct((s, D_MODEL), BF16),
        grid=(s // MERGE_TM,),
        in_specs=[
            pl.BlockSpec(tile, row),
            pl.BlockSpec(tile, row),
            pl.BlockSpec(tile, row),
            pl.BlockSpec(tile, lambda i: (i, B_SG_LRU // D_MODEL)),
            pl.BlockSpec(tile, row),
            pl.BlockSpec(tile, lambda i: (i, B_GM_LRU // D_MODEL)),
            pl.BlockSpec(tile, lambda i: (i, B_GM_ATT // D_MODEL)),
            resident(c_f.shape, const),
            resident(c_b.shape, const),
            resident((D_MODEL, D_MODEL), const),
            resident((D_MODEL, D_MODEL), const),
        ],
        out_specs=pl.BlockSpec(tile, row),
        compiler_params=pltpu.CompilerParams(
            dimension_semantics=("parallel",),
            vmem_limit_bytes=VMEM_LIMIT),
        name="merge",
    )(h_loc, p_f, p_b, zb, y_b, zb, zb, c_f, c_b, wa, wb)


OUT_TM = 512


def _out_kernel(m_ref, x_ref, wo_ref, nw_ref, o_ref):
    y = jnp.dot(m_ref[...], wo_ref[...], preferred_element_type=F32)
    ms = jnp.mean(y * y, axis=-1, keepdims=True)
    o_ref[...] = x_ref[...] + y * lax.rsqrt(ms + NORM_EPS) * nw_ref[...]


def _out_proj(merged, x2, wo, norm_w):
    s = x2.shape[0]
    row = lambda i: (i, 0)
    const = lambda i: (0, 0)
    return pl.pallas_call(
        _out_kernel,
        out_shape=jax.ShapeDtypeStruct((s, D_MODEL), F32),
        grid=(s // OUT_TM,),
        in_specs=[
            pl.BlockSpec((OUT_TM, D_MODEL), row),
            pl.BlockSpec((OUT_TM, D_MODEL), row),
            pl.BlockSpec((D_MODEL, D_MODEL), const, pipeline_mode=pl.Buffered(1)),
            pl.BlockSpec((1, D_MODEL), const),
        ],
        out_specs=pl.BlockSpec((OUT_TM, D_MODEL), row),
        compiler_params=pltpu.CompilerParams(
            dimension_semantics=("parallel",),
            vmem_limit_bytes=VMEM_LIMIT),
        name="out_proj",
    )(merged, x2, wo, norm_w)


def _rope_inputs():
    inv_freq = ROPE_THETA ** (-jnp.arange(0, ROT_DIM, 2, dtype=F32) / ROT_DIM)
    freq = jnp.concatenate([inv_freq, inv_freq, jnp.zeros((LANES - ROT_DIM,), F32)])[None, :]
    ang = jnp.arange(INPROJ_TM, dtype=F32)[:, None] * freq
    return freq, jnp.cos(ang), jnp.sin(ang)


def _gate_weights(w_r, w_i):
    w = jnp.concatenate([w_r[0], w_i[0], w_r[1], w_i[1]], axis=-1)
    return (0.5 * w).astype(BF16)


def _lru_params(conv_w, conv_b, b_r, b_i, lam):
    rows = jnp.concatenate([
        conv_w, conv_b[None, :],
        0.5 * jnp.stack([b_r[0], b_i[0], b_r[1], b_i[1]]),
        lam,
        jnp.zeros((P_ROWS - P_LAMBDA - 2, D_MODEL), F32)], axis=0)
    return rows.reshape(P_ROWS, LRU_BLOCKS, LANES).transpose(1, 0, 2)


def kernel(x, norm_pre_w, w_in, conv_w, conv_b, lru_w_r, lru_b_r, lru_w_i, lru_b_i,
           lru_lambda, attn_sink, w_proj_a, w_proj_b, w_out, norm_post_w):
    bsz, s, d = x.shape
    depth = w_in.shape[0]
    rope_inputs = _rope_inputs()
    outs = []
    for bi in range(bsz):
        xb = x[bi]
        for l in range(depth):
            zb, h_loc, p_f, p_b, ef, pef, eb, peb = _inproj(
                xb, norm_pre_w[l][None, :], w_in[l].astype(BF16), rope_inputs,
                _lru_params(conv_w[l], conv_b[l], lru_b_r[l], lru_b_i[l], lru_lambda[l]),
                _gate_weights(lru_w_r[l], lru_w_i[l]))
            c_f, c_b = _lru_carry(ef, pef, eb, peb)
            y_b = _attention(zb, attn_sink[l])
            merged = _merge(h_loc, p_f, p_b, y_b, zb, c_f, c_b,
                            w_proj_a[l].astype(BF16), w_proj_b[l].astype(BF16))
            xb = _out_proj(merged, xb, w_out[l].astype(BF16), norm_post_w[l][None, :])
        outs.append(xb)
    return jnp.stack(outs, axis=0)
```

```python
import functools
import math

import jax
import jax.numpy as jnp
from jax import lax
from jax.experimental import pallas as pl
from jax.experimental.pallas import tpu as pltpu

F32 = jnp.float32
BF16 = jnp.bfloat16

D_MODEL = 2048
LRU_BLOCK_DIM = 128
LRU_BLOCKS = D_MODEL // LRU_BLOCK_DIM
LRU_C = 8.0
N_HEADS = 16
N_KV_HEADS = 4
GROUP = N_HEADS // N_KV_HEADS
HEAD_DIM = 128
KV_WIDTH = N_KV_HEADS * HEAD_DIM
BLOCK = 128
WINDOW = 128
ROPE_THETA = 500000.0
ROT_DIM = HEAD_DIM // 4
NORM_EPS = 1e-6
MASK_VALUE = -1e30
LOG2E = math.log2(math.e)

B_SG_LRU, B_SG_ATT, B_GM_LRU, B_GM_ATT, B_Q, B_K, B_V = (
    0, 2048, 4096, 6144, 8192, 10240, 10752)
B_WIDTH = 11264

SUBLANES = 8
BF16_SUBLANES = 16
LANES = 128
VMEM_LIMIT = 56 * 1024 * 1024


def _sigmoid(x):
    return 0.5 * jnp.tanh(0.5 * x) + 0.5


def _silu(x):
    hx = 0.5 * x
    return hx * jnp.tanh(hx) + hx


def _sqrt_nonneg(x):
    return jnp.where(x > 0.0, x * lax.rsqrt(x), 0.0)


INPROJ_TM = 1024
INPROJ_TN = 1024
ROPE_ROWS = 256
NORM_ROWS = 64
XN_ROWS = INPROJ_TM + BF16_SUBLANES
J_U, J_SILU, J_SIG, J_Q, J_KV, J_END = 0, 2, 6, 10, 12, 13

LRU_SUB = 128
LRU_CW = 256
LRU_NK = LRU_CW // LRU_BLOCK_DIM
LRU_STEPS = D_MODEL // LRU_CW
SLABS_PER_J = INPROJ_TN // LANES
CONV_LEFT = 2
U_ROWS = (LRU_SUB + 3) * SUBLANES
CONV_TILE = 128
COEFF_TILE = 64
GATE_COLS = 4 * LRU_BLOCK_DIM
P_CONV_W, P_CONV_B, P_BIAS, P_LAMBDA, P_ROWS = 0, 4, 5, 9, 16


def _weight_block(j):
    return jnp.where(j < 4, j, jnp.where(j < 10, j + 3, j - 6))


def _rope_tables(freq_ref, rc_ref, rs_ref, chunk, scale):
    lane = lax.broadcasted_iota(jnp.int32, (1, LANES), 1)
    base = (chunk * INPROJ_TM).astype(F32) * freq_ref[...]
    cb, sb = jnp.cos(base) * scale, jnp.sin(base) * scale
    sign = jnp.where(lane < ROT_DIM // 2, -1.0, jnp.where(lane < ROT_DIM, 1.0, 0.0))
    rc, rs = rc_ref[...], rs_ref[...]
    cos_t = cb * rc - sb * rs
    sin_t = (sign * sb) * rc + (sign * cb) * rs
    return cos_t, sin_t


def _rope(t, cos_t, sin_t):
    half = ROT_DIM // 2
    lane = lax.broadcasted_iota(jnp.int32, t.shape, 1)
    partner = jnp.where(lane < half,
                        pltpu.roll(t, HEAD_DIM - half, axis=1),
                        pltpu.roll(t, half, axis=1))
    return t * cos_t + partner * sin_t


def _lru_conv(kp, lp_ref, u_s, uc_s, ucb_s):
    for k in range(LRU_NK):
        kk = kp * LRU_NK + k
        par = lp_ref.at[kk]
        u_k = u_s.at[kk]
        for t0 in range(0, INPROJ_TM, CONV_TILE):
            y = par[P_CONV_B:P_CONV_B + 1, :]
            for tap in range(4):
                lo = t0 + tap * SUBLANES
                y = y + u_k[lo:lo + CONV_TILE, :] * par[P_CONV_W + tap:P_CONV_W + tap + 1, :]
            uc_s[k, t0:t0 + CONV_TILE, :] = y
            ucb_s[k, t0:t0 + CONV_TILE, :] = y.astype(BF16)


def _lru_gates(kp, wg_ref, ucb_s, z_s):
    for k in range(LRU_NK):
        z = jnp.dot(ucb_s[k], wg_ref[kp * LRU_NK + k], preferred_element_type=F32)
        for c, zc_s in enumerate(z_s):
            zc_s[k] = z[:, c * LANES:(c + 1) * LANES]


def _lru_coeffs(kp, lp_ref, uc_s, ab_s):
    for k in range(LRU_NK):
        par = lp_ref.at[kp * LRU_NK + k]
        for d in range(2):
            a_s, b_s = ab_s[2 * d], ab_s[2 * d + 1]
            nl = -par[P_LAMBDA + d:P_LAMBDA + d + 1, :]
            softplus = jnp.maximum(nl, 0.0) + jnp.log1p(jnp.exp(-jnp.abs(nl)))
            dq = -0.25 * LRU_C * softplus
            br_h = par[P_BIAS + 2 * d:P_BIAS + 2 * d + 1, :]
            bi_h = par[P_BIAS + 2 * d + 1:P_BIAS + 2 * d + 2, :]
            for t0 in range(0, INPROJ_TM, COEFF_TILE):
                rows = slice(t0, t0 + COEFF_TILE)
                tr = jnp.tanh(a_s[k, rows, :] + br_h)
                th = jnp.tanh(tr * dq + dq)
                inv = 1.0 / (1.0 - th)
                root = _sqrt_nonneg(-th)
                ti = jnp.tanh(b_s[k, rows, :] + bi_h)
                a_s[k, rows, :] = (1.0 + th) * inv
                b_s[k, rows, :] = (root * inv) * ((ti + 1.0) * uc_s[k, rows, :])


def _lru_scan(ab_s, h_ref, pf_ref, pb_ref, ef_ref, pef_ref, eb_ref, peb_ref):
    zif_s, zrf_s, zib_s, zrb_s = ab_s
    half = LRU_SUB // 2

    def step(k, d, r, h, p):
        rows = slice(r * SUBLANES, (r + 1) * SUBLANES)
        zr_s, zi_s = (zrf_s, zif_s) if d == 0 else (zrb_s, zib_s)
        a = zi_s[k, rows, :]
        h = a * h + zr_s[k, rows, :]
        p = a * p
        crossed = (r >= half) if d == 0 else (r < half)
        if crossed:
            other = zrb_s if d == 0 else zrf_s
            zrf_s[k, rows, :] = h + other[k, rows, :]
        else:
            zr_s[k, rows, :] = h
        zi_s[k, rows, :] = p
        return h, p

    zero = jnp.zeros((SUBLANES, LANES), F32)
    one = jnp.ones((SUBLANES, LANES), F32)
    state = [[zero, one, zero, one] for _ in range(LRU_NK)]
    for it in range(LRU_SUB):
        for k in range(LRU_NK):
            hf, pf, hb, pb = state[k]
            hf, pf = step(k, 0, it, hf, pf)
            hb, pb = step(k, 1, LRU_SUB - 1 - it, hb, pb)
            state[k] = [hf, pf, hb, pb]

    for k in range(LRU_NK):
        cols = slice(k * LANES, (k + 1) * LANES)
        hf, pf, hb, pb = state[k]
        ef_ref[:, cols] = hf
        pef_ref[:, cols] = pf
        eb_ref[:, cols] = hb
        peb_ref[:, cols] = pb
        for s in range(SUBLANES):
            rows = slice(s * LRU_SUB, (s + 1) * LRU_SUB)
            sel = pl.ds(s, LRU_SUB, stride=SUBLANES)
            h_ref[rows, cols] = zrf_s[k, sel, :]
            pf_ref[rows, cols] = zif_s[k, sel, :].astype(BF16)
            pb_ref[rows, cols] = zib_s[k, sel, :].astype(BF16)


def _inproj_kernel(n_chunks, x_ref, xh_ref, nw_ref, w_ref, freq_ref, rc_ref, rs_ref, lp_ref, wg_ref,
                   b_ref, h_ref, pf_ref, pb_ref, ef_ref, pef_ref, eb_ref, peb_ref,
                   x_buf, x_sem, xn_ref, u_s, uprev_s, uc_s, ucb_s, zrf_s, zif_s, zrb_s, zib_s):
    i = pl.program_id(0)
    j = pl.program_id(1)
    z_s = (zrf_s, zif_s, zrb_s, zib_s)
    lru_outs = (h_ref, pf_ref, pb_ref, ef_ref, pef_ref, eb_ref, peb_ref)

    def x_copy(chunk):
        rows = pl.ds(pl.multiple_of(chunk * INPROJ_TM, INPROJ_TM), INPROJ_TM)
        return pltpu.make_async_copy(x_ref.at[rows, :], x_buf, x_sem)

    @pl.when((i == 0) & (j == 0))
    def _():
        x_copy(0).start()

    @pl.when((j == 1) & (i < n_chunks - 1))
    def _():
        x_copy(i + 1).start()

    def project(lo=0, hi=INPROJ_TM):
        return jnp.dot(xn_ref[lo:hi, :], w_ref[...], preferred_element_type=F32)

    def project_with_lru(epilogue):
        kp = j - J_SILU
        _lru_gates(kp, wg_ref, ucb_s, z_s)
        _lru_coeffs(kp, lp_ref, uc_s, z_s)
        b_ref[...] = epilogue(project()).astype(BF16)
        _lru_scan(z_s, *lru_outs)
        _lru_conv(jnp.minimum(kp + 1, LRU_STEPS - 1), lp_ref, u_s, uc_s, ucb_s)

    def normalise(xv):
        ms = jnp.mean(xv * xv, axis=-1, keepdims=True)
        return (xv * lax.rsqrt(ms + NORM_EPS) * nw_ref[...]).astype(BF16)

    @pl.when(j == 0)
    def _():
        x_copy(i).wait()

        def body(r, carry):
            rows = pl.ds(pl.multiple_of(r * NORM_ROWS, NORM_ROWS), NORM_ROWS)
            xn_ref[rows, :] = normalise(x_buf[rows, :])
            return carry
        lax.fori_loop(0, INPROJ_TM // NORM_ROWS, body, 0, unroll=4)
        xn_ref[INPROJ_TM:, :] = normalise(xh_ref[...])

    @pl.when(j < J_SILU)
    def _():
        sub = lax.broadcasted_iota(jnp.int32, (SUBLANES, LANES), 0)
        first_slab = CONV_LEFT * SUBLANES
        for lo in range(0, INPROJ_TM, ROPE_ROWS):
            last = lo + ROPE_ROWS == INPROJ_TM
            acc = project(lo, XN_ROWS if last else lo + ROPE_ROWS)
            for k in range(SLABS_PER_J):
                u_k = u_s.at[j * SLABS_PER_J + k]
                for s in range(lo // LRU_SUB, (lo + ROPE_ROWS) // LRU_SUB):
                    u_k[pl.ds(first_slab + s, LRU_SUB, stride=SUBLANES), :] = (
                        acc[s * LRU_SUB - lo:(s + 1) * LRU_SUB - lo, k * LANES:(k + 1) * LANES])
        tail = INPROJ_TM - lo
        for k in range(SLABS_PER_J):
            kk = j * SLABS_PER_J + k
            cols = slice(k * LANES, (k + 1) * LANES)
            u_k = u_s.at[kk]
            prev = uprev_s[kk]
            prev1 = jnp.where(i > 0, prev[SUBLANES - 1:SUBLANES, :], 0.0)
            prev2 = jnp.where(i > 0, prev[SUBLANES - 2:SUBLANES - 1, :], 0.0)
            nxt = jnp.where(i < n_chunks - 1, acc[tail:tail + 1, cols], 0.0)
            last1 = u_k[pl.ds(first_slab + (LRU_SUB - 1) * SUBLANES, SUBLANES), :]
            last2 = u_k[pl.ds(first_slab + (LRU_SUB - 2) * SUBLANES, SUBLANES), :]
            head = u_k[pl.ds(first_slab, SUBLANES), :]
            u_k[pl.ds(SUBLANES, SUBLANES), :] = jnp.where(sub == 0, prev1, pltpu.roll(last1, 1, axis=0))
            u_k[pl.ds(0, SUBLANES), :] = jnp.where(sub == 0, prev2, pltpu.roll(last2, 1, axis=0))
            u_k[pl.ds(first_slab + LRU_SUB * SUBLANES, SUBLANES), :] = jnp.where(
                sub == SUBLANES - 1, nxt, pltpu.roll(head, SUBLANES - 1, axis=0))
            uprev_s[kk] = acc[tail - SUBLANES:tail, cols]
        _lru_conv(0, lp_ref, u_s, uc_s, ucb_s)

    @pl.when((j >= J_SILU) & (j < J_SIG))
    def _():
        project_with_lru(_silu)

    @pl.when((j >= J_SIG) & (j < J_Q))
    def _():
        project_with_lru(_sigmoid)

    @pl.when((j >= J_Q) & (j < J_KV))
    def _():
        cos_t, sin_t = _rope_tables(freq_ref, rc_ref, rs_ref, i, HEAD_DIM ** -0.5 * LOG2E)
        for lo in range(0, INPROJ_TM, ROPE_ROWS):
            rows = slice(lo, lo + ROPE_ROWS)
            acc = project(lo, lo + ROPE_ROWS)
            for hb in range(INPROJ_TN // HEAD_DIM):
                cols = slice(hb * HEAD_DIM, (hb + 1) * HEAD_DIM)
                b_ref[rows, cols] = _rope(acc[:, cols], cos_t[rows], sin_t[rows]).astype(BF16)

    @pl.when(j == J_KV)
    def _():
        cos_t, sin_t = _rope_tables(freq_ref, rc_ref, rs_ref, i, 1.0)
        for lo in range(0, INPROJ_TM, ROPE_ROWS):
            rows = slice(lo, lo + ROPE_ROWS)
            acc = project(lo, lo + ROPE_ROWS)
            for hb in range(N_KV_HEADS):
                cols = slice(hb * HEAD_DIM, (hb + 1) * HEAD_DIM)
                b_ref[rows, cols] = _rope(acc[:, cols], cos_t[rows], sin_t[rows]).astype(BF16)
            b_ref[rows, KV_WIDTH:] = acc[:, KV_WIDTH:].astype(BF16)


def _inproj(x2, norm_w, w_in_bf16, rope_inputs, lru_params, w_gates):
    s = x2.shape[0]
    n_chunks = s // INPROJ_TM
    n_sub = s // LRU_SUB
    halo_per_chunk = INPROJ_TM // BF16_SUBLANES
    n_halo = s // BF16_SUBLANES
    lru_col = lambda i, j: (i, jnp.clip(j - J_SILU, 0, LRU_STEPS - 1))
    const2 = lambda i, j: (0, 0)
    const3 = lambda i, j: (0, 0, 0)
    scan_scratch = pltpu.VMEM((LRU_NK, INPROJ_TM, LANES), F32)
    return pl.pallas_call(
        functools.partial(_inproj_kernel, n_chunks),
        out_shape=(jax.ShapeDtypeStruct((s, B_WIDTH), BF16),
                   jax.ShapeDtypeStruct((s, D_MODEL), F32),
                   jax.ShapeDtypeStruct((s, D_MODEL), BF16),
                   jax.ShapeDtypeStruct((s, D_MODEL), BF16),
                   jax.ShapeDtypeStruct((n_sub, D_MODEL), F32),
                   jax.ShapeDtypeStruct((n_sub, D_MODEL), F32),
                   jax.ShapeDtypeStruct((n_sub, D_MODEL), F32),
                   jax.ShapeDtypeStruct((n_sub, D_MODEL), F32)),
        grid=(n_chunks, J_END),
        in_specs=[
            pl.BlockSpec(memory_space=pl.ANY),
            pl.BlockSpec((BF16_SUBLANES, D_MODEL),
                         lambda i, j: (jnp.minimum((i + 1) * halo_per_chunk, n_halo - 1), 0)),
            pl.BlockSpec((1, D_MODEL), const2),
            pl.BlockSpec((D_MODEL, INPROJ_TN), lambda i, j: (0, _weight_block(j))),
            pl.BlockSpec((1, LANES), const2),
            pl.BlockSpec((INPROJ_TM, LANES), const2, pipeline_mode=pl.Buffered(1)),
            pl.BlockSpec((INPROJ_TM, LANES), const2, pipeline_mode=pl.Buffered(1)),
            pl.BlockSpec((LRU_BLOCKS, P_ROWS, LANES), const3),
            pl.BlockSpec((LRU_BLOCKS, LRU_BLOCK_DIM, GATE_COLS), const3),
        ],
        out_specs=(
            pl.BlockSpec((INPROJ_TM, INPROJ_TN), lambda i, j: (i, jnp.maximum(j - J_SILU, 0))),
            pl.BlockSpec((INPROJ_TM, LRU_CW), lru_col),
            pl.BlockSpec((INPROJ_TM, LRU_CW), lru_col),
            pl.BlockSpec((INPROJ_TM, LRU_CW), lru_col),
            pl.BlockSpec((SUBLANES, LRU_CW), lru_col),
            pl.BlockSpec((SUBLANES, LRU_CW), lru_col),
            pl.BlockSpec((SUBLANES, LRU_CW), lru_col),
            pl.BlockSpec((SUBLANES, LRU_CW), lru_col),
        ),
        scratch_shapes=[
            pltpu.VMEM((INPROJ_TM, D_MODEL), F32),
            pltpu.SemaphoreType.DMA(()),
            pltpu.VMEM((XN_ROWS, D_MODEL), BF16),
            pltpu.VMEM((LRU_BLOCKS, U_ROWS, LANES), F32),
            pltpu.VMEM((LRU_BLOCKS, SUBLANES, LANES), F32),
            scan_scratch,
            pltpu.VMEM((LRU_NK, INPROJ_TM, LANES), BF16),
            scan_scratch, scan_scratch, scan_scratch, scan_scratch,
        ],
        compiler_params=pltpu.CompilerParams(
            dimension_semantics=("arbitrary", "arbitrary"),
            vmem_limit_bytes=VMEM_LIMIT),
        name="inproj",
    )(x2, x2, norm_w, w_in_bf16, *rope_inputs, lru_params, w_gates)


def _lru_carry_kernel(ef_ref, pef_ref, eb_ref, peb_ref, cf_ref, cbk_ref):
    n_sub = ef_ref.shape[0]

    def fwd(g, carry):
        row = pl.ds(g, 1)
        cf_ref[row, :] = carry
        return ef_ref[row, :] + pef_ref[row, :] * carry

    def bwd(i, carry):
        row = pl.ds(n_sub - 1 - i, 1)
        cbk_ref[row, :] = carry
        return eb_ref[row, :] + peb_ref[row, :] * carry

    zero = jnp.zeros((1, ef_ref.shape[1]), F32)
    lax.fori_loop(0, n_sub, fwd, zero)
    lax.fori_loop(0, n_sub, bwd, zero)


def _lru_carry(ef, pef, eb, peb):
    shape = jax.ShapeDtypeStruct(ef.shape, F32)
    return pl.pallas_call(
        _lru_carry_kernel,
        out_shape=(shape, shape),
        name="lru_carry",
    )(ef, pef, eb, peb)


ATT_QB = 4
ATT_TQ = ATT_QB * BLOCK


def _attn_kernel(n_steps, sink_ref, q_ref, kl_ref, kc_ref, kr_ref, vl_ref, vc_ref, vr_ref,
                 sg_ref, o_ref):
    step = pl.program_id(0)
    k_ext = jnp.concatenate([kl_ref[...], kc_ref[...], kr_ref[...]], axis=0)
    v_ext = jnp.concatenate([vl_ref[...], vc_ref[...], vr_ref[...]], axis=0)

    q_idx = lax.broadcasted_iota(jnp.int32, (BLOCK, BLOCK), 0)
    s_idx = lax.broadcasted_iota(jnp.int32, (BLOCK, BLOCK), 1)
    tri_left = jnp.where(s_idx >= q_idx, 0.0, MASK_VALUE)
    tri_right = jnp.where(s_idx <= q_idx, 0.0, MASK_VALUE)

    for kh in range(N_KV_HEADS):
        kv_cols = slice(kh * HEAD_DIM, (kh + 1) * HEAD_DIM)
        sinks = [sink_ref[kh * GROUP + g] * LOG2E for g in range(GROUP)]
        for qb in range(ATT_QB):
            bias_l, bias_r = tri_left, tri_right
            if qb == 0:
                bias_l = jnp.where(step > 0, tri_left, MASK_VALUE)
            if qb == ATT_QB - 1:
                bias_r = jnp.where(step < n_steps - 1, tri_right, MASK_VALUE)
            q_rows = slice(qb * BLOCK, (qb + 1) * BLOCK)
            k_rows = slice(qb * BLOCK, (qb + 3) * BLOCK)
            q = jnp.concatenate(
                [q_ref[q_rows, (kh * GROUP + g) * HEAD_DIM:(kh * GROUP + g + 1) * HEAD_DIM]
                 for g in range(GROUP)], axis=0)
            scores = lax.dot_general(q, k_ext[k_rows, kv_cols], (((1,), (1,)), ((), ())),
                                     preferred_element_type=F32)
            probs, denoms = [], []
            for g in range(GROUP):
                head = scores[g * BLOCK:(g + 1) * BLOCK, :]
                s_l = head[:, :BLOCK] + bias_l
                s_c = head[:, BLOCK:2 * BLOCK]
                s_r = head[:, 2 * BLOCK:] + bias_r
                m = jnp.max(jnp.maximum(jnp.maximum(s_l, s_c), s_r), axis=-1, keepdims=True)
                m = jnp.maximum(m, sinks[g])
                e_l, e_c, e_r = jnp.exp2(s_l - m), jnp.exp2(s_c - m), jnp.exp2(s_r - m)
                denoms.append(jnp.sum(e_l + e_c + e_r, axis=-1, keepdims=True) + jnp.exp2(sinks[g] - m))
                probs.append(jnp.concatenate(
                    [e_l.astype(BF16), e_c.astype(BF16), e_r.astype(BF16)], axis=1))
            out = jnp.dot(jnp.concatenate(probs, axis=0), v_ext[k_rows, kv_cols],
                          preferred_element_type=F32)
            for g in range(GROUP):
                cols = slice((kh * GROUP + g) * HEAD_DIM, (kh * GROUP + g + 1) * HEAD_DIM)
                gate = sg_ref[q_rows, cols].astype(F32)
                o_ref[q_rows, cols] = (out[g * BLOCK:(g + 1) * BLOCK, :] / denoms[g] * gate
                                       ).astype(o_ref.dtype)


def _attention(zb, sink):
    s = zb.shape[0]
    n_steps = s // ATT_TQ
    n_blocks = s // BLOCK
    aw = N_HEADS * HEAD_DIM

    def left(t):
        return jnp.maximum(t * ATT_QB - 1, 0)

    def right(t):
        return jnp.minimum((t + 1) * ATT_QB, n_blocks - 1)

    kcol = B_K // KV_WIDTH
    vcol = B_V // KV_WIDTH
    in_specs = [
        pl.BlockSpec(memory_space=pltpu.SMEM),
        pl.BlockSpec((ATT_TQ, aw), lambda t: (t, B_Q // aw)),
        pl.BlockSpec((BLOCK, KV_WIDTH), lambda t: (left(t), kcol)),
        pl.BlockSpec((ATT_TQ, KV_WIDTH), lambda t: (t, kcol)),
        pl.BlockSpec((BLOCK, KV_WIDTH), lambda t: (right(t), kcol)),
        pl.BlockSpec((BLOCK, KV_WIDTH), lambda t: (left(t), vcol)),
        pl.BlockSpec((ATT_TQ, KV_WIDTH), lambda t: (t, vcol)),
        pl.BlockSpec((BLOCK, KV_WIDTH), lambda t: (right(t), vcol)),
        pl.BlockSpec((ATT_TQ, aw), lambda t: (t, B_SG_ATT // aw)),
    ]
    return pl.pallas_call(
        functools.partial(_attn_kernel, n_steps),
        out_shape=jax.ShapeDtypeStruct((s, aw), BF16),
        grid=(n_steps,),
        in_specs=in_specs,
        out_specs=pl.BlockSpec((ATT_TQ, aw), lambda t: (t, 0)),
        compiler_params=pltpu.CompilerParams(
            dimension_semantics=("parallel",),
            vmem_limit_bytes=VMEM_LIMIT),
        name="attn",
    )(sink, zb, zb, zb, zb, zb, zb, zb, zb)


MERGE_TM = 256


def _merge_kernel(h_ref, pf_ref, pb_ref, sg_ref, yb_ref, ga_ref, gb_ref, cf_ref, cbk_ref,
                  wa_ref, wb_ref, o_ref):
    i = pl.program_id(0)
    parts = []
    for q in range(MERGE_TM // LRU_SUB):
        g = i * (MERGE_TM // LRU_SUB) + q
        rows = slice(q * LRU_SUB, (q + 1) * LRU_SUB)
        h = (h_ref[rows, :]
             + pf_ref[rows, :].astype(F32) * cf_ref[pl.ds(g, 1), :]
             + pb_ref[rows, :].astype(F32) * cbk_ref[pl.ds(g, 1), :])
        parts.append((h * sg_ref[rows, :].astype(F32)).astype(BF16))
    y_a = jnp.concatenate(parts, axis=0)
    pa = jnp.dot(y_a, wa_ref[...], preferred_element_type=F32)
    pb = jnp.dot(yb_ref[...], wb_ref[...], preferred_element_type=F32)
    o_ref[...] = (ga_ref[...].astype(F32) * pa + gb_ref[...].astype(F32) * pb).astype(BF16)


def _merge(h_loc, p_f, p_b, y_b, zb, c_f, c_b, wa, wb):
    s = h_loc.shape[0]
    row = lambda i: (i, 0)
    const = lambda i: (0, 0)
    resident = functools.partial(pl.BlockSpec, pipeline_mode=pl.Buffered(1))
    tile = (MERGE_TM, D_MODEL)
    return pl.pallas_call(
        _merge_kernel,
        out_shape=jax.ShapeDtypeStruct((s, D_MODEL), BF16),
        grid=(s // MERGE_TM,),
        in_specs=[
            pl.BlockSpec(tile, row),
            pl.BlockSpec(tile, row),
            pl.BlockSpec(tile, row),
            pl.BlockSpec(tile, lambda i: (i, B_SG_LRU // D_MODEL)),
            pl.BlockSpec(tile, row),
            pl.BlockSpec(tile, lambda i: (i, B_GM_LRU // D_MODEL)),
            pl.BlockSpec(tile, lambda i: (i, B_GM_ATT // D_MODEL)),
            resident(c_f.shape, const),
            resident(c_b.shape, const),
            resident((D_MODEL, D_MODEL), const),
            resident((D_MODEL, D_MODEL), const),
        ],
        out_specs=pl.BlockSpec(tile, row),
        compiler_params=pltpu.CompilerParams(
            dimension_semantics=("parallel",),
            vmem_limit_bytes=VMEM_LIMIT),
        name="merge",
    )(h_loc, p_f, p_b, zb, y_b, zb, zb, c_f, c_b, wa, wb)


OUT_TM = 512


def _out_kernel(m_ref, x_ref, wo_ref, nw_ref, o_ref):
    y = jnp.dot(m_ref[...], wo_ref[...], preferred_element_type=F32)
    ms = jnp.mean(y * y, axis=-1, keepdims=True)
    o_ref[...] = x_ref[...] + y * lax.rsqrt(ms + NORM_EPS) * nw_ref[...]


def _out_proj(merged, x2, wo, norm_w):
    s = x2.shape[0]
    row = lambda i: (i, 0)
    const = lambda i: (0, 0)
    return pl.pallas_call(
        _out_kernel,
        out_shape=jax.ShapeDtypeStruct((s, D_MODEL), F32),
        grid=(s // OUT_TM,),
        in_specs=[
            pl.BlockSpec((OUT_TM, D_MODEL), row),
            pl.BlockSpec((OUT_TM, D_MODEL), row),
            pl.BlockSpec((D_MODEL, D_MODEL), const, pipeline_mode=pl.Buffered(1)),
            pl.BlockSpec((1, D_MODEL), const),
        ],
        out_specs=pl.BlockSpec((OUT_TM, D_MODEL), row),
        compiler_params=pltpu.CompilerParams(
            dimension_semantics=("parallel",),
            vmem_limit_bytes=VMEM_LIMIT),
        name="out_proj",
    )(merged, x2, wo, norm_w)


def _rope_inputs():
    inv_freq = ROPE_THETA ** (-jnp.arange(0, ROT_DIM, 2, dtype=F32) / ROT_DIM)
    freq = jnp.concatenate([inv_freq, inv_freq, jnp.zeros((LANES - ROT_DIM,), F32)])[None, :]
    ang = jnp.arange(INPROJ_TM, dtype=F32)[:, None] * freq
    return freq, jnp.cos(ang), jnp.sin(ang)


def _gate_weights(w_r, w_i):
    w = jnp.concatenate([w_r[0], w_i[0], w_r[1], w_i[1]], axis=-1)
    return (0.5 * w).astype(BF16)


def _lru_params(conv_w, conv_b, b_r, b_i, lam):
    rows = jnp.concatenate([
        conv_w, conv_b[None, :],
        0.5 * jnp.stack([b_r[0], b_i[0], b_r[1], b_i[1]]),
        lam,
        jnp.zeros((P_ROWS - P_LAMBDA - 2, D_MODEL), F32)], axis=0)
    return rows.reshape(P_ROWS, LRU_BLOCKS, LANES).transpose(1, 0, 2)


def kernel(x, norm_pre_w, w_in, conv_w, conv_b, lru_w_r, lru_b_r, lru_w_i, lru_b_i,
           lru_lambda, attn_sink, w_proj_a, w_proj_b, w_out, norm_post_w):
    bsz, s, d = x.shape
    depth = w_in.shape[0]
    rope_inputs = _rope_inputs()
    outs = []
    for bi in range(bsz):
        xb = x[bi]
        for l in range(depth):
            zb, h_loc, p_f, p_b, ef, pef, eb, peb = _inproj(
                xb, norm_pre_w[l][None, :], w_in[l].astype(BF16), rope_inputs,
                _lru_params(conv_w[l], conv_b[l], lru_b_r[l], lru_b_i[l], lru_lambda[l]),
                _gate_weights(lru_w_r[l], lru_w_i[l]))
            c_f, c_b = _lru_carry(ef, pef, eb, peb)
            y_b = _attention(zb, attn_sink[l])
            merged = _merge(h_loc, p_f, p_b, y_b, zb, c_f, c_b,
                            w_proj_a[l].astype(BF16), w_proj_b[l].astype(BF16))
            xb = _out_proj(merged, xb, w_out[l].astype(BF16), norm_post_w[l][None, :])
        outs.append(xb)
    return jnp.stack(outs, axis=0)
```

```python
import functools
import math

import jax
import jax.numpy as jnp
from jax import lax
from jax.experimental import pallas as pl
from jax.experimental.pallas import tpu as pltpu

F32 = jnp.float32
BF16 = jnp.bfloat16

D_MODEL = 2048
LRU_BLOCK_DIM = 128
LRU_BLOCKS = D_MODEL // LRU_BLOCK_DIM
LRU_C = 8.0
N_HEADS = 16
N_KV_HEADS = 4
GROUP = N_HEADS // N_KV_HEADS
HEAD_DIM = 128
KV_WIDTH = N_KV_HEADS * HEAD_DIM
BLOCK = 128
WINDOW = 128
ROPE_THETA = 500000.0
ROT_DIM = HEAD_DIM // 4
NORM_EPS = 1e-6
MASK_VALUE = -1e30
LOG2E = math.log2(math.e)

B_SG_LRU, B_SG_ATT, B_GM_LRU, B_GM_ATT, B_Q, B_K, B_V = (
    0, 2048, 4096, 6144, 8192, 10240, 10752)
B_WIDTH = 11264

SUBLANES = 8
BF16_SUBLANES = 16
LANES = 128
VMEM_LIMIT = 56 * 1024 * 1024


def _sigmoid(x):
    return 0.5 * jnp.tanh(0.5 * x) + 0.5


def _silu(x):
    hx = 0.5 * x
    return hx * jnp.tanh(hx) + hx


def _sqrt_nonneg(x):
    return jnp.where(x > 0.0, x * lax.rsqrt(x), 0.0)


INPROJ_TM = 1024
INPROJ_TN = 1024
ROPE_ROWS = 256
NORM_ROWS = 64
XN_ROWS = INPROJ_TM + BF16_SUBLANES
J_U, J_SILU, J_SIG, J_Q, J_KV, J_END = 0, 2, 6, 10, 12, 13

LRU_SUB = 128
LRU_CW = 256
LRU_NK = LRU_CW // LRU_BLOCK_DIM
LRU_STEPS = D_MODEL // LRU_CW
SLABS_PER_J = INPROJ_TN // LANES
CONV_LEFT = 2
U_ROWS = (LRU_SUB + 3) * SUBLANES
CONV_TILE = 128
COEFF_TILE = 64
GATE_COLS = 4 * LRU_BLOCK_DIM
P_CONV_W, P_CONV_B, P_BIAS, P_LAMBDA, P_ROWS = 0, 4, 5, 9, 16


def _weight_block(j):
    return jnp.where(j < 4, j, jnp.where(j < 10, j + 3, j - 6))


def _rope_tables(freq_ref, rc_ref, rs_ref, chunk, scale):
    lane = lax.broadcasted_iota(jnp.int32, (1, LANES), 1)
    base = (chunk * INPROJ_TM).astype(F32) * freq_ref[...]
    cb, sb = jnp.cos(base) * scale, jnp.sin(base) * scale
    sign = jnp.where(lane < ROT_DIM // 2, -1.0, jnp.where(lane < ROT_DIM, 1.0, 0.0))
    rc, rs = rc_ref[...], rs_ref[...]
    cos_t = cb * rc - sb * rs
    sin_t = (sign * sb) * rc + (sign * cb) * rs
    return cos_t, sin_t


def _rope(t, cos_t, sin_t):
    half = ROT_DIM // 2
    lane = lax.broadcasted_iota(jnp.int32, t.shape, 1)
    partner = jnp.where(lane < half,
                        pltpu.roll(t, HEAD_DIM - half, axis=1),
                        pltpu.roll(t, half, axis=1))
    return t * cos_t + partner * sin_t


def _lru_conv(kp, lp_ref, u_s, uc_s, ucb_s):
    for k in range(LRU_NK):
        kk = kp * LRU_NK + k
        par = lp_ref.at[kk]
        u_k = u_s.at[kk]
        for t0 in range(0, INPROJ_TM, CONV_TILE):
            y = par[P_CONV_B:P_CONV_B + 1, :]
            for tap in range(4):
                lo = t0 + tap * SUBLANES
                y = y + u_k[lo:lo + CONV_TILE, :] * par[P_CONV_W + tap:P_CONV_W + tap + 1, :]
            uc_s[k, t0:t0 + CONV_TILE, :] = y
            ucb_s[k, t0:t0 + CONV_TILE, :] = y.astype(BF16)


def _lru_gates(kp, wg_ref, ucb_s, z_s):
    for k in range(LRU_NK):
        z = jnp.dot(ucb_s[k], wg_ref[kp * LRU_NK + k], preferred_element_type=F32)
        for c, zc_s in enumerate(z_s):
            zc_s[k] = z[:, c * LANES:(c + 1) * LANES]


def _lru_coeffs(kp, lp_ref, uc_s, ab_s):
    for k in range(LRU_NK):
        par = lp_ref.at[kp * LRU_NK + k]
        for d in range(2):
            a_s, b_s = ab_s[2 * d], ab_s[2 * d + 1]
            nl = -par[P_LAMBDA + d:P_LAMBDA + d + 1, :]
            softplus = jnp.maximum(nl, 0.0) + jnp.log1p(jnp.exp(-jnp.abs(nl)))
            dq = -0.25 * LRU_C * softplus
            br_h = par[P_BIAS + 2 * d:P_BIAS + 2 * d + 1, :]
            bi_h = par[P_BIAS + 2 * d + 1:P_BIAS + 2 * d + 2, :]
            for t0 in range(0, INPROJ_TM, COEFF_TILE):
                rows = slice(t0, t0 + COEFF_TILE)
                tr = jnp.tanh(a_s[k, rows, :] + br_h)
                th = jnp.tanh(tr * dq + dq)
                inv = 1.0 / (1.0 - th)
                root = _sqrt_nonneg(-th)
                ti = jnp.tanh(b_s[k, rows, :] + bi_h)
                a_s[k, rows, :] = (1.0 + th) * inv
                b_s[k, rows, :] = (root * inv) * ((ti + 1.0) * uc_s[k, rows, :])


def _lru_scan(ab_s, h_ref, pf_ref, pb_ref, ef_ref, pef_ref, eb_ref, peb_ref):
    zif_s, zrf_s, zib_s, zrb_s = ab_s
    half = LRU_SUB // 2

    def step(k, d, r, h, p):
        rows = slice(r * SUBLANES, (r + 1) * SUBLANES)
        zr_s, zi_s = (zrf_s, zif_s) if d == 0 else (zrb_s, zib_s)
        a = zi_s[k, rows, :]
        h = a * h + zr_s[k, rows, :]
        p = a * p
        crossed = (r >= half) if d == 0 else (r < half)
        if crossed:
            other = zrb_s if d == 0 else zrf_s
            zrf_s[k, rows, :] = h + other[k, rows, :]
        else:
            zr_s[k, rows, :] = h
        zi_s[k, rows, :] = p
        return h, p

    zero = jnp.zeros((SUBLANES, LANES), F32)
    one = jnp.ones((SUBLANES, LANES), F32)
    state = [[zero, one, zero, one] for _ in range(LRU_NK)]
    for it in range(LRU_SUB):
        for k in range(LRU_NK):
            hf, pf, hb, pb = state[k]
            hf, pf = step(k, 0, it, hf, pf)
            hb, pb = step(k, 1, LRU_SUB - 1 - it, hb, pb)
            state[k] = [hf, pf, hb, pb]

    for k in range(LRU_NK):
        cols = slice(k * LANES, (k + 1) * LANES)
        hf, pf, hb, pb = state[k]
        ef_ref[:, cols] = hf
        pef_ref[:, cols] = pf
        eb_ref[:, cols] = hb
        peb_ref[:, cols] = pb
        for s in range(SUBLANES):
            rows = slice(s * LRU_SUB, (s + 1) * LRU_SUB)
            sel = pl.ds(s, LRU_SUB, stride=SUBLANES)
            h_ref[rows, cols] = zrf_s[k, sel, :]
            pf_ref[rows, cols] = zif_s[k, sel, :].astype(BF16)
            pb_ref[rows, cols] = zib_s[k, sel, :].astype(BF16)


def _inproj_kernel(n_chunks, x_ref, xh_ref, nw_ref, w_ref, freq_ref, rc_ref, rs_ref, lp_ref, wg_ref,
                   b_ref, h_ref, pf_ref, pb_ref, ef_ref, pef_ref, eb_ref, peb_ref,
                   x_buf, x_sem, xn_ref, u_s, uprev_s, uc_s, ucb_s, zrf_s, zif_s, zrb_s, zib_s):
    i = pl.program_id(0)
    j = pl.program_id(1)
    z_s = (zrf_s, zif_s, zrb_s, zib_s)
    lru_outs = (h_ref, pf_ref, pb_ref, ef_ref, pef_ref, eb_ref, peb_ref)

    def x_copy(chunk):
        rows = pl.ds(pl.multiple_of(chunk * INPROJ_TM, INPROJ_TM), INPROJ_TM)
        return pltpu.make_async_copy(x_ref.at[rows, :], x_buf, x_sem)

    @pl.when((i == 0) & (j == 0))
    def _():
        x_copy(0).start()

    @pl.when((j == 1) & (i < n_chunks - 1))
    def _():
        x_copy(i + 1).start()

    def project(lo=0, hi=INPROJ_TM):
        return jnp.dot(xn_ref[lo:hi, :], w_ref[...], preferred_element_type=F32)

    def project_with_lru(epilogue):
        kp = j - J_SILU
        _lru_gates(kp, wg_ref, ucb_s, z_s)
        _lru_coeffs(kp, lp_ref, uc_s, z_s)
        b_ref[...] = epilogue(project()).astype(BF16)
        _lru_scan(z_s, *lru_outs)
        _lru_conv(jnp.minimum(kp + 1, LRU_STEPS - 1), lp_ref, u_s, uc_s, ucb_s)

    def normalise(xv):
        ms = jnp.mean(xv * xv, axis=-1, keepdims=True)
        return (xv * lax.rsqrt(ms + NORM_EPS) * nw_ref[...]).astype(BF16)

    @pl.when(j == 0)
    def _():
        x_copy(i).wait()

        def body(r, carry):
            rows = pl.ds(pl.multiple_of(r * NORM_ROWS, NORM_ROWS), NORM_ROWS)
            xn_ref[rows, :] = normalise(x_buf[rows, :])
            return carry
        lax.fori_loop(0, INPROJ_TM // NORM_ROWS, body, 0, unroll=4)
        xn_ref[INPROJ_TM:, :] = normalise(xh_ref[...])

    @pl.when(j < J_SILU)
    def _():
        sub = lax.broadcasted_iota(jnp.int32, (SUBLANES, LANES), 0)
        first_slab = CONV_LEFT * SUBLANES
        for lo in range(0, INPROJ_TM, ROPE_ROWS):
            last = lo + ROPE_ROWS == INPROJ_TM
            acc = project(lo, XN_ROWS if last else lo + ROPE_ROWS)
            for k in range(SLABS_PER_J):
                u_k = u_s.at[j * SLABS_PER_J + k]
                for s in range(lo // LRU_SUB, (lo + ROPE_ROWS) // LRU_SUB):
                    u_k[pl.ds(first_slab + s, LRU_SUB, stride=SUBLANES), :] = (
                        acc[s * LRU_SUB - lo:(s + 1) * LRU_SUB - lo, k * LANES:(k + 1) * LANES])
        tail = INPROJ_TM - lo
        for k in range(SLABS_PER_J):
            kk = j * SLABS_PER_J + k
            cols = slice(k * LANES, (k + 1) * LANES)
            u_k = u_s.at[kk]
            prev = uprev_s[kk]
            prev1 = jnp.where(i > 0, prev[SUBLANES - 1:SUBLANES, :], 0.0)
            prev2 = jnp.where(i > 0, prev[SUBLANES - 2:SUBLANES - 1, :], 0.0)
            nxt = jnp.where(i < n_chunks - 1, acc[tail:tail + 1, cols], 0.0)
            last1 = u_k[pl.ds(first_slab + (LRU_SUB - 1) * SUBLANES, SUBLANES), :]
            last2 = u_k[pl.ds(first_slab + (LRU_SUB - 2) * SUBLANES, SUBLANES), :]
            head = u_k[pl.ds(first_slab, SUBLANES), :]
            u_k[pl.ds(SUBLANES, SUBLANES), :] = jnp.where(sub == 0, prev1, pltpu.roll(last1, 1, axis=0))
            u_k[pl.ds(0, SUBLANES), :] = jnp.where(sub == 0, prev2, pltpu.roll(last2, 1, axis=0))
            u_k[pl.ds(first_slab + LRU_SUB * SUBLANES, SUBLANES), :] = jnp.where(
                sub == SUBLANES - 1, nxt, pltpu.roll(head, SUBLANES - 1, axis=0))
            uprev_s[kk] = acc[tail - SUBLANES:tail, cols]
        _lru_conv(0, lp_ref, u_s, uc_s, ucb_s)

    @pl.when((j >= J_SILU) & (j < J_SIG))
    def _():
        project_with_lru(_silu)

    @pl.when((j >= J_SIG) & (j < J_Q))
    def _():
        project_with_lru(_sigmoid)

    @pl.when((j >= J_Q) & (j < J_KV))
    def _():
        cos_t, sin_t = _rope_tables(freq_ref, rc_ref, rs_ref, i, HEAD_DIM ** -0.5 * LOG2E)
        for lo in range(0, INPROJ_TM, ROPE_ROWS):
            rows = slice(lo, lo + ROPE_ROWS)
            acc = project(lo, lo + ROPE_ROWS)
            for hb in range(INPROJ_TN // HEAD_DIM):
                cols = slice(hb * HEAD_DIM, (hb + 1) * HEAD_DIM)
                b_ref[rows, cols] = _rope(acc[:, cols], cos_t[rows], sin_t[rows]).astype(BF16)

    @pl.when(j == J_KV)
    def _():
        cos_t, sin_t = _rope_tables(freq_ref, rc_ref, rs_ref, i, 1.0)
        for lo in range(0, INPROJ_TM, ROPE_ROWS):
            rows = slice(lo, lo + ROPE_ROWS)
            acc = project(lo, lo + ROPE_ROWS)
            for hb in range(N_KV_HEADS):
                cols = slice(hb * HEAD_DIM, (hb + 1) * HEAD_DIM)
                b_ref[rows, cols] = _rope(acc[:, cols], cos_t[rows], sin_t[rows]).astype(BF16)
            b_ref[rows, KV_WIDTH:] = acc[:, KV_WIDTH:].astype(BF16)


def _inproj(x2, norm_w, w_in_bf16, rope_inputs, lru_params, w_gates):
    s = x2.shape[0]
    n_chunks = s // INPROJ_TM
    n_sub = s // LRU_SUB
    halo_per_chunk = INPROJ_TM // BF16_SUBLANES
    n_halo = s // BF16_SUBLANES
    lru_col = lambda i, j: (i, jnp.clip(j - J_SILU, 0, LRU_STEPS - 1))
    const2 = lambda i, j: (0, 0)
    const3 = lambda i, j: (0, 0, 0)
    scan_scratch = pltpu.VMEM((LRU_NK, INPROJ_TM, LANES), F32)
    return pl.pallas_call(
        functools.partial(_inproj_kernel, n_chunks),
        out_shape=(jax.ShapeDtypeStruct((s, B_WIDTH), BF16),
                   jax.ShapeDtypeStruct((s, D_MODEL), F32),
                   jax.ShapeDtypeStruct((s, D_MODEL), BF16),
                   jax.ShapeDtypeStruct((s, D_MODEL), BF16),
                   jax.ShapeDtypeStruct((n_sub, D_MODEL), F32),
                   jax.ShapeDtypeStruct((n_sub, D_MODEL), F32),
                   jax.ShapeDtypeStruct((n_sub, D_MODEL), F32),
                   jax.ShapeDtypeStruct((n_sub, D_MODEL), F32)),
        grid=(n_chunks, J_END),
        in_specs=[
            pl.BlockSpec(memory_space=pl.ANY),
            pl.BlockSpec((BF16_SUBLANES, D_MODEL),
                         lambda i, j: (jnp.minimum((i + 1) * halo_per_chunk, n_halo - 1), 0)),
            pl.BlockSpec((1, D_MODEL), const2),
            pl.BlockSpec((D_MODEL, INPROJ_TN), lambda i, j: (0, _weight_block(j))),
            pl.BlockSpec((1, LANES), const2),
            pl.BlockSpec((INPROJ_TM, LANES), const2, pipeline_mode=pl.Buffered(1)),
            pl.BlockSpec((INPROJ_TM, LANES), const2, pipeline_mode=pl.Buffered(1)),
            pl.BlockSpec((LRU_BLOCKS, P_ROWS, LANES), const3),
            pl.BlockSpec((LRU_BLOCKS, LRU_BLOCK_DIM, GATE_COLS), const3),
        ],
        out_specs=(
            pl.BlockSpec((INPROJ_TM, INPROJ_TN), lambda i, j: (i, jnp.maximum(j - J_SILU, 0))),
            pl.BlockSpec((INPROJ_TM, LRU_CW), lru_col),
            pl.BlockSpec((INPROJ_TM, LRU_CW), lru_col),
            pl.BlockSpec((INPROJ_TM, LRU_CW), lru_col),
            pl.BlockSpec((SUBLANES, LRU_CW), lru_col),
            pl.BlockSpec((SUBLANES, LRU_CW), lru_col),
            pl.BlockSpec((SUBLANES, LRU_CW), lru_col),
            pl.BlockSpec((SUBLANES, LRU_CW), lru_col),
        ),
        scratch_shapes=[
            pltpu.VMEM((INPROJ_TM, D_MODEL), F32),
            pltpu.SemaphoreType.DMA(()),
            pltpu.VMEM((XN_ROWS, D_MODEL), BF16),
            pltpu.VMEM((LRU_BLOCKS, U_ROWS, LANES), F32),
            pltpu.VMEM((LRU_BLOCKS, SUBLANES, LANES), F32),
            scan_scratch,
            pltpu.VMEM((LRU_NK, INPROJ_TM, LANES), BF16),
            scan_scratch, scan_scratch, scan_scratch, scan_scratch,
        ],
        compiler_params=pltpu.CompilerParams(
            dimension_semantics=("arbitrary", "arbitrary"),
            vmem_limit_bytes=VMEM_LIMIT),
        name="inproj",
    )(x2, x2, norm_w, w_in_bf16, *rope_inputs, lru_params, w_gates)


def _lru_carry_kernel(ef_ref, pef_ref, eb_ref, peb_ref, cf_ref, cbk_ref):
    n_sub = ef_ref.shape[0]

    def fwd(g, carry):
        row = pl.ds(g, 1)
        cf_ref[row, :] = carry
        return ef_ref[row, :] + pef_ref[row, :] * carry

    def bwd(i, carry):
        row = pl.ds(n_sub - 1 - i, 1)
        cbk_ref[row, :] = carry
        return eb_ref[row, :] + peb_ref[row, :] * carry

    zero = jnp.zeros((1, ef_ref.shape[1]), F32)
    lax.fori_loop(0, n_sub, fwd, zero)
    lax.fori_loop(0, n_sub, bwd, zero)


def _lru_carry(ef, pef, eb, peb):
    shape = jax.ShapeDtypeStruct(ef.shape, F32)
    return pl.pallas_call(
        _lru_carry_kernel,
        out_shape=(shape, shape),
        name="lru_carry",
    )(ef, pef, eb, peb)


ATT_QB = 4
ATT_TQ = ATT_QB * BLOCK


def _attn_kernel(n_steps, sink_ref, q_ref, kl_ref, kc_ref, kr_ref, vl_ref, vc_ref, vr_ref,
                 sg_ref, o_ref):
    step = pl.program_id(0)
    k_ext = jnp.concatenate([kl_ref[...], kc_ref[...], kr_ref[...]], axis=0)
    v_ext = jnp.concatenate([vl_ref[...], vc_ref[...], vr_ref[...]], axis=0)

    q_idx = lax.broadcasted_iota(jnp.int32, (BLOCK, BLOCK), 0)
    s_idx = lax.broadcasted_iota(jnp.int32, (BLOCK, BLOCK), 1)
    tri_left = jnp.where(s_idx >= q_idx, 0.0, MASK_VALUE)
    tri_right = jnp.where(s_idx <= q_idx, 0.0, MASK_VALUE)

    for kh in range(N_KV_HEADS):
        kv_cols = slice(kh * HEAD_DIM, (kh + 1) * HEAD_DIM)
        sinks = [sink_ref[kh * GROUP + g] * LOG2E for g in range(GROUP)]
        for qb in range(ATT_QB):
            bias_l, bias_r = tri_left, tri_right
            if qb == 0:
                bias_l = jnp.where(step > 0, tri_left, MASK_VALUE)
            if qb == ATT_QB - 1:
                bias_r = jnp.where(step < n_steps - 1, tri_right, MASK_VALUE)
            q_rows = slice(qb * BLOCK, (qb + 1) * BLOCK)
            k_rows = slice(qb * BLOCK, (qb + 3) * BLOCK)
            q = jnp.concatenate(
                [q_ref[q_rows, (kh * GROUP + g) * HEAD_DIM:(kh * GROUP + g + 1) * HEAD_DIM]
                 for g in range(GROUP)], axis=0)
            scores = lax.dot_general(q, k_ext[k_rows, kv_cols], (((1,), (1,)), ((), ())),
                                     preferred_element_type=F32)
            probs, denoms = [], []
            for g in range(GROUP):
                head = scores[g * BLOCK:(g + 1) * BLOCK, :]
                s_l = head[:, :BLOCK] + bias_l
                s_c = head[:, BLOCK:2 * BLOCK]
                s_r = head[:, 2 * BLOCK:] + bias_r
                m = jnp.max(jnp.maximum(jnp.maximum(s_l, s_c), s_r), axis=-1, keepdims=True)
                m = jnp.maximum(m, sinks[g])
                e_l, e_c, e_r = jnp.exp2(s_l - m), jnp.exp2(s_c - m), jnp.exp2(s_r - m)
                denoms.append(jnp.sum(e_l + e_c + e_r, axis=-1, keepdims=True) + jnp.exp2(sinks[g] - m))
                probs.append(jnp.concatenate(
                    [e_l.astype(BF16), e_c.astype(BF16), e_r.astype(BF16)], axis=1))
            out = jnp.dot(jnp.concatenate(probs, axis=0), v_ext[k_rows, kv_cols],
                          preferred_element_type=F32)
            for g in range(GROUP):
                cols = slice((kh * GROUP + g) * HEAD_DIM, (kh * GROUP + g + 1) * HEAD_DIM)
                gate = sg_ref[q_rows, cols].astype(F32)
                o_ref[q_rows, cols] = (out[g * BLOCK:(g + 1) * BLOCK, :] / denoms[g] * gate
                                       ).astype(o_ref.dtype)


def _attention(zb, sink):
    s = zb.shape[0]
    n_steps = s // ATT_TQ
    n_blocks = s // BLOCK
    aw = N_HEADS * HEAD_DIM

    def left(t):
        return jnp.maximum(t * ATT_QB - 1, 0)

    def right(t):
        return jnp.minimum((t + 1) * ATT_QB, n_blocks - 1)

    kcol = B_K // KV_WIDTH
    vcol = B_V // KV_WIDTH
    in_specs = [
        pl.BlockSpec(memory_space=pltpu.SMEM),
        pl.BlockSpec((ATT_TQ, aw), lambda t: (t, B_Q // aw)),
        pl.BlockSpec((BLOCK, KV_WIDTH), lambda t: (left(t), kcol)),
        pl.BlockSpec((ATT_TQ, KV_WIDTH), lambda t: (t, kcol)),
        pl.BlockSpec((BLOCK, KV_WIDTH), lambda t: (right(t), kcol)),
        pl.BlockSpec((BLOCK, KV_WIDTH), lambda t: (left(t), vcol)),
        pl.BlockSpec((ATT_TQ, KV_WIDTH), lambda t: (t, vcol)),
        pl.BlockSpec((BLOCK, KV_WIDTH), lambda t: (right(t), vcol)),
        pl.BlockSpec((ATT_TQ, aw), lambda t: (t, B_SG_ATT // aw)),
    ]
    return pl.pallas_call(
        functools.partial(_attn_kernel, n_steps),
        out_shape=jax.ShapeDtypeStruct((s, aw), BF16),
        grid=(n_steps,),
        in_specs=in_specs,
        out_specs=pl.BlockSpec((ATT_TQ, aw), lambda t: (t, 0)),
        compiler_params=pltpu.CompilerParams(
            dimension_semantics=("parallel",),
            vmem_limit_bytes=VMEM_LIMIT),
        name="attn",
    )(sink, zb, zb, zb, zb, zb, zb, zb, zb)


MERGE_TM = 256
MERGE_TN = 512


def _merge_kernel(h_ref, pf_ref, pb_ref, sg_ref, yb_ref, ga_ref, gb_ref, cf_ref, cbk_ref,
                  wa_ref, wb_ref, o_ref):
    i = pl.program_id(0)

    def branch_b(cols):
        return jnp.dot(yb_ref[...], wb_ref[:, cols], preferred_element_type=F32)

    first = slice(0, MERGE_TN)
    pb = branch_b(first)
    parts = []
    for q in range(MERGE_TM // LRU_SUB):
        g = i * (MERGE_TM // LRU_SUB) + q
        rows = slice(q * LRU_SUB, (q + 1) * LRU_SUB)
        h = (h_ref[rows, :]
             + pf_ref[rows, :].astype(F32) * cf_ref[pl.ds(g, 1), :]
             + pb_ref[rows, :].astype(F32) * cbk_ref[pl.ds(g, 1), :])
        parts.append((h * sg_ref[rows, :].astype(F32)).astype(BF16))
    y_a = jnp.concatenate(parts, axis=0)
    for c0 in range(0, D_MODEL, MERGE_TN):
        cols = slice(c0, c0 + MERGE_TN)
        if c0 > 0:
            pb = branch_b(cols)
        pa = jnp.dot(y_a, wa_ref[:, cols], preferred_element_type=F32)
        o_ref[:, cols] = (ga_ref[:, cols].astype(F32) * pa
                          + gb_ref[:, cols].astype(F32) * pb).astype(BF16)


def _merge(h_loc, p_f, p_b, y_b, zb, c_f, c_b, wa, wb):
    s = h_loc.shape[0]
    row = lambda i: (i, 0)
    const = lambda i: (0, 0)
    resident = functools.partial(pl.BlockSpec, pipeline_mode=pl.Buffered(1))
    tile = (MERGE_TM, D_MODEL)
    return pl.pallas_call(
        _merge_kernel,
        out_shape=jax.ShapeDtypeStruct((s, D_MODEL), BF16),
        grid=(s // MERGE_TM,),
        in_specs=[
            pl.BlockSpec(tile, row),
            pl.BlockSpec(tile, row),
            pl.BlockSpec(tile, row),
            pl.BlockSpec(tile, lambda i: (i, B_SG_LRU // D_MODEL)),
            pl.BlockSpec(tile, row),
            pl.BlockSpec(tile, lambda i: (i, B_GM_LRU // D_MODEL)),
            pl.BlockSpec(tile, lambda i: (i, B_GM_ATT // D_MODEL)),
            resident(c_f.shape, const),
            resident(c_b.shape, const),
            resident((D_MODEL, D_MODEL), const),
            resident((D_MODEL, D_MODEL), const),
        ],
        out_specs=pl.BlockSpec(tile, row),
        compiler_params=pltpu.CompilerParams(
            dimension_semantics=("parallel",),
            vmem_limit_bytes=VMEM_LIMIT),
        name="merge",
    )(h_loc, p_f, p_b, zb, y_b, zb, zb, c_f, c_b, wa, wb)


OUT_TM = 512


def _out_kernel(m_ref, x_ref, wo_ref, nw_ref, o_ref):
    y = jnp.dot(m_ref[...], wo_ref[...], preferred_element_type=F32)
    ms = jnp.mean(y * y, axis=-1, keepdims=True)
    o_ref[...] = x_ref[...] + y * lax.rsqrt(ms + NORM_EPS) * nw_ref[...]


def _out_proj(merged, x2, wo, norm_w):
    s = x2.shape[0]
    row = lambda i: (i, 0)
    const = lambda i: (0, 0)
    return pl.pallas_call(
        _out_kernel,
        out_shape=jax.ShapeDtypeStruct((s, D_MODEL), F32),
        grid=(s // OUT_TM,),
        in_specs=[
            pl.BlockSpec((OUT_TM, D_MODEL), row),
            pl.BlockSpec((OUT_TM, D_MODEL), row),
            pl.BlockSpec((D_MODEL, D_MODEL), const, pipeline_mode=pl.Buffered(1)),
            pl.BlockSpec((1, D_MODEL), const),
        ],
        out_specs=pl.BlockSpec((OUT_TM, D_MODEL), row),
        compiler_params=pltpu.CompilerParams(
            dimension_semantics=("parallel",),
            vmem_limit_bytes=VMEM_LIMIT),
        name="out_proj",
    )(merged, x2, wo, norm_w)


def _rope_inputs():
    inv_freq = ROPE_THETA ** (-jnp.arange(0, ROT_DIM, 2, dtype=F32) / ROT_DIM)
    freq = jnp.concatenate([inv_freq, inv_freq, jnp.zeros((LANES - ROT_DIM,), F32)])[None, :]
    ang = jnp.arange(INPROJ_TM, dtype=F32)[:, None] * freq
    return freq, jnp.cos(ang), jnp.sin(ang)


def _gate_weights(w_r, w_i):
    w = jnp.concatenate([w_r[0], w_i[0], w_r[1], w_i[1]], axis=-1)
    return (0.5 * w).astype(BF16)


def _lru_params(conv_w, conv_b, b_r, b_i, lam):
    rows = jnp.concatenate([
        conv_w, conv_b[None, :],
        0.5 * jnp.stack([b_r[0], b_i[0], b_r[1], b_i[1]]),
        lam,
        jnp.zeros((P_ROWS - P_LAMBDA - 2, D_MODEL), F32)], axis=0)
    return rows.reshape(P_ROWS, LRU_BLOCKS, LANES).transpose(1, 0, 2)


def kernel(x, norm_pre_w, w_in, conv_w, conv_b, lru_w_r, lru_b_r, lru_w_i, lru_b_i,
           lru_lambda, attn_sink, w_proj_a, w_proj_b, w_out, norm_post_w):
    bsz, s, d = x.shape
    depth = w_in.shape[0]
    rope_inputs = _rope_inputs()
    outs = []
    for bi in range(bsz):
        xb = x[bi]
        for l in range(depth):
            zb, h_loc, p_f, p_b, ef, pef, eb, peb = _inproj(
                xb, norm_pre_w[l][None, :], w_in[l].astype(BF16), rope_inputs,
                _lru_params(conv_w[l], conv_b[l], lru_b_r[l], lru_b_i[l], lru_lambda[l]),
                _gate_weights(lru_w_r[l], lru_w_i[l]))
            c_f, c_b = _lru_carry(ef, pef, eb, peb)
            y_b = _attention(zb, attn_sink[l])
            merged = _merge(h_loc, p_f, p_b, y_b, zb, c_f, c_b,
                            w_proj_a[l].astype(BF16), w_proj_b[l].astype(BF16))
            xb = _out_proj(merged, xb, w_out[l].astype(BF16), norm_post_w[l][None, :])
        outs.append(xb)
    return jnp.stack(outs, axis=0)
```

```python
import functools
import math

import jax
import jax.numpy as jnp
from jax import lax
from jax.experimental import pallas as pl
from jax.experimental.pallas import tpu as pltpu

F32 = jnp.float32
BF16 = jnp.bfloat16

D_MODEL = 2048
LRU_BLOCK_DIM = 128
LRU_BLOCKS = D_MODEL // LRU_BLOCK_DIM
LRU_C = 8.0
N_HEADS = 16
N_KV_HEADS = 4
GROUP = N_HEADS // N_KV_HEADS
HEAD_DIM = 128
KV_WIDTH = N_KV_HEADS * HEAD_DIM
BLOCK = 128
WINDOW = 128
ROPE_THETA = 500000.0
ROT_DIM = HEAD_DIM // 4
NORM_EPS = 1e-6
MASK_VALUE = -1e30
LOG2E = math.log2(math.e)

B_SG_LRU, B_SG_ATT, B_GM_LRU, B_GM_ATT, B_Q, B_K, B_V = (
    0, 2048, 4096, 6144, 8192, 10240, 10752)
B_WIDTH = 11264

SUBLANES = 8
BF16_SUBLANES = 16
LANES = 128
VMEM_LIMIT = 56 * 1024 * 1024


def _sigmoid(x):
    return 0.5 * jnp.tanh(0.5 * x) + 0.5


def _silu(x):
    hx = 0.5 * x
    return hx * jnp.tanh(hx) + hx


def _sqrt_nonneg(x):
    return jnp.where(x > 0.0, x * lax.rsqrt(x), 0.0)


INPROJ_TM = 1024
INPROJ_TN = 1024
ROPE_ROWS = 256
NORM_ROWS = 64
XN_ROWS = INPROJ_TM + BF16_SUBLANES
J_U, J_SILU, J_SIG, J_Q, J_KV, J_END = 0, 2, 6, 10, 12, 13

LRU_SUB = 128
LRU_CW = 256
LRU_NK = LRU_CW // LRU_BLOCK_DIM
LRU_STEPS = D_MODEL // LRU_CW
SLABS_PER_J = INPROJ_TN // LANES
CONV_LEFT = 2
U_ROWS = (LRU_SUB + 3) * SUBLANES
CONV_TILE = 128
COEFF_TILE = 64
GATE_COLS = 4 * LRU_BLOCK_DIM
P_CONV_W, P_CONV_B, P_BIAS, P_LAMBDA, P_ROWS = 0, 4, 5, 9, 16


def _weight_block(j):
    return jnp.where(j < 4, j, jnp.where(j < 10, j + 3, j - 6))


def _rope_tables(freq_ref, rc_ref, rs_ref, chunk, scale):
    lane = lax.broadcasted_iota(jnp.int32, (1, LANES), 1)
    base = (chunk * INPROJ_TM).astype(F32) * freq_ref[...]
    cb, sb = jnp.cos(base) * scale, jnp.sin(base) * scale
    sign = jnp.where(lane < ROT_DIM // 2, -1.0, jnp.where(lane < ROT_DIM, 1.0, 0.0))
    rc, rs = rc_ref[...], rs_ref[...]
    cos_t = cb * rc - sb * rs
    sin_t = (sign * sb) * rc + (sign * cb) * rs
    return cos_t, sin_t


def _rope(t, cos_t, sin_t):
    half = ROT_DIM // 2
    lane = lax.broadcasted_iota(jnp.int32, t.shape, 1)
    partner = jnp.where(lane < half,
                        pltpu.roll(t, HEAD_DIM - half, axis=1),
                        pltpu.roll(t, half, axis=1))
    return t * cos_t + partner * sin_t


def _lru_conv(kp, lp_ref, u_s, uc_s, ucb_s):
    for k in range(LRU_NK):
        kk = kp * LRU_NK + k
        par = lp_ref.at[kk]
        u_k = u_s.at[kk]
        for t0 in range(0, INPROJ_TM, CONV_TILE):
            y = par[P_CONV_B:P_CONV_B + 1, :]
            for tap in range(4):
                lo = t0 + tap * SUBLANES
                y = y + u_k[lo:lo + CONV_TILE, :] * par[P_CONV_W + tap:P_CONV_W + tap + 1, :]
            uc_s[k, t0:t0 + CONV_TILE, :] = y
            ucb_s[k, t0:t0 + CONV_TILE, :] = y.astype(BF16)


def _lru_gates(kp, wg_ref, ucb_s, z_s):
    for k in range(LRU_NK):
        z = jnp.dot(ucb_s[k], wg_ref[kp * LRU_NK + k], preferred_element_type=F32)
        for c, zc_s in enumerate(z_s):
            zc_s[k] = z[:, c * LANES:(c + 1) * LANES]


def _zero_after(x):
    bits = lax.shift_right_logical(pltpu.bitcast(x, jnp.uint32), jnp.uint32(16))
    return pltpu.bitcast(lax.shift_right_logical(bits, jnp.uint32(16)), F32)


def _lru_coeffs(kp, lp_ref, uc_s, ab_s, pace):
    tiles_per_col = (INPROJ_TM // 2) // COEFF_TILE
    for k in range(LRU_NK):
        par = lp_ref.at[kp * LRU_NK + k]
        for d in range(2):
            a_s, b_s = ab_s[2 * d], ab_s[2 * d + 1]
            pace_half = pace[k]
            nl = -par[P_LAMBDA + d:P_LAMBDA + d + 1, :]
            softplus = jnp.maximum(nl, 0.0) + jnp.log1p(jnp.exp(-jnp.abs(nl)))
            dq = -0.25 * LRU_C * softplus
            br_h = par[P_BIAS + 2 * d:P_BIAS + 2 * d + 1, :]
            bi_h = par[P_BIAS + 2 * d + 1:P_BIAS + 2 * d + 2, :]
            for t0 in range(0, INPROJ_TM, COEFF_TILE):
                rows = slice(t0, t0 + COEFF_TILE)
                t = t0 // COEFF_TILE
                p_rows = slice((t % tiles_per_col) * COEFF_TILE, (t % tiles_per_col + 1) * COEFF_TILE)
                p_col = (d * 2 + t // tiles_per_col) * LANES
                zero = _zero_after(pace_half[p_rows, p_col:p_col + LANES])
                tr = jnp.tanh(a_s[k, rows, :] + br_h + zero)
                th = jnp.tanh(tr * dq + dq)
                inv = 1.0 / (1.0 - th)
                root = _sqrt_nonneg(-th)
                ti = jnp.tanh(b_s[k, rows, :] + bi_h)
                a_s[k, rows, :] = (1.0 + th) * inv
                b_s[k, rows, :] = (root * inv) * ((ti + 1.0) * uc_s[k, rows, :])


def _lru_scan(ab_s, h_ref, pf_ref, pb_ref, ef_ref, pef_ref, eb_ref, peb_ref):
    zif_s, zrf_s, zib_s, zrb_s = ab_s
    half = LRU_SUB // 2

    def step(k, d, r, h, p):
        rows = slice(r * SUBLANES, (r + 1) * SUBLANES)
        zr_s, zi_s = (zrf_s, zif_s) if d == 0 else (zrb_s, zib_s)
        a = zi_s[k, rows, :]
        h = a * h + zr_s[k, rows, :]
        p = a * p
        crossed = (r >= half) if d == 0 else (r < half)
        if crossed:
            other = zrb_s if d == 0 else zrf_s
            zrf_s[k, rows, :] = h + other[k, rows, :]
        else:
            zr_s[k, rows, :] = h
        zi_s[k, rows, :] = p
        return h, p

    zero = jnp.zeros((SUBLANES, LANES), F32)
    one = jnp.ones((SUBLANES, LANES), F32)
    state = [[zero, one, zero, one] for _ in range(LRU_NK)]
    for it in range(LRU_SUB):
        for k in range(LRU_NK):
            hf, pf, hb, pb = state[k]
            hf, pf = step(k, 0, it, hf, pf)
            hb, pb = step(k, 1, LRU_SUB - 1 - it, hb, pb)
            state[k] = [hf, pf, hb, pb]

    for k in range(LRU_NK):
        cols = slice(k * LANES, (k + 1) * LANES)
        hf, pf, hb, pb = state[k]
        ef_ref[:, cols] = hf
        pef_ref[:, cols] = pf
        eb_ref[:, cols] = hb
        peb_ref[:, cols] = pb
        for s in range(SUBLANES):
            rows = slice(s * LRU_SUB, (s + 1) * LRU_SUB)
            sel = pl.ds(s, LRU_SUB, stride=SUBLANES)
            h_ref[rows, cols] = zrf_s[k, sel, :]
            pf_ref[rows, cols] = zif_s[k, sel, :].astype(BF16)
            pb_ref[rows, cols] = zib_s[k, sel, :].astype(BF16)


def _inproj_kernel(n_chunks, x_ref, xh_ref, nw_ref, w_ref, freq_ref, rc_ref, rs_ref, lp_ref, wg_ref,
                   b_ref, h_ref, pf_ref, pb_ref, ef_ref, pef_ref, eb_ref, peb_ref,
                   x_buf, x_sem, xn_ref, u_s, uprev_s, uc_s, ucb_s, zrf_s, zif_s, zrb_s, zib_s):
    i = pl.program_id(0)
    j = pl.program_id(1)
    z_s = (zrf_s, zif_s, zrb_s, zib_s)
    lru_outs = (h_ref, pf_ref, pb_ref, ef_ref, pef_ref, eb_ref, peb_ref)

    def x_copy(chunk):
        rows = pl.ds(pl.multiple_of(chunk * INPROJ_TM, INPROJ_TM), INPROJ_TM)
        return pltpu.make_async_copy(x_ref.at[rows, :], x_buf, x_sem)

    @pl.when((i == 0) & (j == 0))
    def _():
        x_copy(0).start()

    @pl.when((j == 1) & (i < n_chunks - 1))
    def _():
        x_copy(i + 1).start()

    def project(lo=0, hi=INPROJ_TM):
        return jnp.dot(xn_ref[lo:hi, :], w_ref[...], preferred_element_type=F32)

    def project_with_lru(epilogue):
        kp = j - J_SILU
        _lru_gates(kp, wg_ref, ucb_s, z_s)
        half = INPROJ_TM // 2
        acc = [project(0, half), project(half, INPROJ_TM)]
        b_ref[0:half, :] = epilogue(acc[0]).astype(BF16)
        b_ref[half:, :] = epilogue(acc[1]).astype(BF16)
        _lru_coeffs(kp, lp_ref, uc_s, z_s, acc)
        _lru_scan(z_s, *lru_outs)
        _lru_conv(jnp.minimum(kp + 1, LRU_STEPS - 1), lp_ref, u_s, uc_s, ucb_s)

    def normalise(xv):
        ms = jnp.mean(xv * xv, axis=-1, keepdims=True)
        return (xv * lax.rsqrt(ms + NORM_EPS) * nw_ref[...]).astype(BF16)

    @pl.when(j == 0)
    def _():
        x_copy(i).wait()

        def body(r, carry):
            rows = pl.ds(pl.multiple_of(r * NORM_ROWS, NORM_ROWS), NORM_ROWS)
            xn_ref[rows, :] = normalise(x_buf[rows, :])
            return carry
        lax.fori_loop(0, INPROJ_TM // NORM_ROWS, body, 0, unroll=4)
        xn_ref[INPROJ_TM:, :] = normalise(xh_ref[...])

    @pl.when(j < J_SILU)
    def _():
        sub = lax.broadcasted_iota(jnp.int32, (SUBLANES, LANES), 0)
        first_slab = CONV_LEFT * SUBLANES
        for lo in range(0, INPROJ_TM, ROPE_ROWS):
            last = lo + ROPE_ROWS == INPROJ_TM
            acc = project(lo, XN_ROWS if last else lo + ROPE_ROWS)
            for k in range(SLABS_PER_J):
                u_k = u_s.at[j * SLABS_PER_J + k]
                for s in range(lo // LRU_SUB, (lo + ROPE_ROWS) // LRU_SUB):
                    u_k[pl.ds(first_slab + s, LRU_SUB, stride=SUBLANES), :] = (
                        acc[s * LRU_SUB - lo:(s + 1) * LRU_SUB - lo, k * LANES:(k + 1) * LANES])
        tail = INPROJ_TM - lo
        for k in range(SLABS_PER_J):
            kk = j * SLABS_PER_J + k
            cols = slice(k * LANES, (k + 1) * LANES)
            u_k = u_s.at[kk]
            prev = uprev_s[kk]
            prev1 = jnp.where(i > 0, prev[SUBLANES - 1:SUBLANES, :], 0.0)
            prev2 = jnp.where(i > 0, prev[SUBLANES - 2:SUBLANES - 1, :], 0.0)
            nxt = jnp.where(i < n_chunks - 1, acc[tail:tail + 1, cols], 0.0)
            last1 = u_k[pl.ds(first_slab + (LRU_SUB - 1) * SUBLANES, SUBLANES), :]
            last2 = u_k[pl.ds(first_slab + (LRU_SUB - 2) * SUBLANES, SUBLANES), :]
            head = u_k[pl.ds(first_slab, SUBLANES), :]
            u_k[pl.ds(SUBLANES, SUBLANES), :] = jnp.where(sub == 0, prev1, pltpu.roll(last1, 1, axis=0))
            u_k[pl.ds(0, SUBLANES), :] = jnp.where(sub == 0, prev2, pltpu.roll(last2, 1, axis=0))
            u_k[pl.ds(first_slab + LRU_SUB * SUBLANES, SUBLANES), :] = jnp.where(
                sub == SUBLANES - 1, nxt, pltpu.roll(head, SUBLANES - 1, axis=0))
            uprev_s[kk] = acc[tail - SUBLANES:tail, cols]
        _lru_conv(0, lp_ref, u_s, uc_s, ucb_s)

    @pl.when((j >= J_SILU) & (j < J_SIG))
    def _():
        project_with_lru(_silu)

    @pl.when((j >= J_SIG) & (j < J_Q))
    def _():
        project_with_lru(_sigmoid)

    @pl.when((j >= J_Q) & (j < J_KV))
    def _():
        cos_t, sin_t = _rope_tables(freq_ref, rc_ref, rs_ref, i, HEAD_DIM ** -0.5 * LOG2E)
        for lo in range(0, INPROJ_TM, ROPE_ROWS):
            rows = slice(lo, lo + ROPE_ROWS)
            acc = project(lo, lo + ROPE_ROWS)
            for hb in range(INPROJ_TN // HEAD_DIM):
                cols = slice(hb * HEAD_DIM, (hb + 1) * HEAD_DIM)
                b_ref[rows, cols] = _rope(acc[:, cols], cos_t[rows], sin_t[rows]).astype(BF16)

    @pl.when(j == J_KV)
    def _():
        cos_t, sin_t = _rope_tables(freq_ref, rc_ref, rs_ref, i, 1.0)
        for lo in range(0, INPROJ_TM, ROPE_ROWS):
            rows = slice(lo, lo + ROPE_ROWS)
            acc = project(lo, lo + ROPE_ROWS)
            for hb in range(N_KV_HEADS):
                cols = slice(hb * HEAD_DIM, (hb + 1) * HEAD_DIM)
                b_ref[rows, cols] = _rope(acc[:, cols], cos_t[rows], sin_t[rows]).astype(BF16)
            b_ref[rows, KV_WIDTH:] = acc[:, KV_WIDTH:].astype(BF16)


def _inproj(x2, norm_w, w_in_bf16, rope_inputs, lru_params, w_gates):
    s = x2.shape[0]
    n_chunks = s // INPROJ_TM
    n_sub = s // LRU_SUB
    halo_per_chunk = INPROJ_TM // BF16_SUBLANES
    n_halo = s // BF16_SUBLANES
    lru_col = lambda i, j: (i, jnp.clip(j - J_SILU, 0, LRU_STEPS - 1))
    const2 = lambda i, j: (0, 0)
    const3 = lambda i, j: (0, 0, 0)
    scan_scratch = pltpu.VMEM((LRU_NK, INPROJ_TM, LANES), F32)
    return pl.pallas_call(
        functools.partial(_inproj_kernel, n_chunks),
        out_shape=(jax.ShapeDtypeStruct((s, B_WIDTH), BF16),
                   jax.ShapeDtypeStruct((s, D_MODEL), F32),
                   jax.ShapeDtypeStruct((s, D_MODEL), BF16),
                   jax.ShapeDtypeStruct((s, D_MODEL), BF16),
                   jax.ShapeDtypeStruct((n_sub, D_MODEL), F32),
                   jax.ShapeDtypeStruct((n_sub, D_MODEL), F32),
                   jax.ShapeDtypeStruct((n_sub, D_MODEL), F32),
                   jax.ShapeDtypeStruct((n_sub, D_MODEL), F32)),
        grid=(n_chunks, J_END),
        in_specs=[
            pl.BlockSpec(memory_space=pl.ANY),
            pl.BlockSpec((BF16_SUBLANES, D_MODEL),
                         lambda i, j: (jnp.minimum((i + 1) * halo_per_chunk, n_halo - 1), 0)),
            pl.BlockSpec((1, D_MODEL), const2),
            pl.BlockSpec((D_MODEL, INPROJ_TN), lambda i, j: (0, _weight_block(j))),
            pl.BlockSpec((1, LANES), const2),
            pl.BlockSpec((INPROJ_TM, LANES), const2, pipeline_mode=pl.Buffered(1)),
            pl.BlockSpec((INPROJ_TM, LANES), const2, pipeline_mode=pl.Buffered(1)),
            pl.BlockSpec((LRU_BLOCKS, P_ROWS, LANES), const3),
            pl.BlockSpec((LRU_BLOCKS, LRU_BLOCK_DIM, GATE_COLS), const3),
        ],
        out_specs=(
            pl.BlockSpec((INPROJ_TM, INPROJ_TN), lambda i, j: (i, jnp.maximum(j - J_SILU, 0))),
            pl.BlockSpec((INPROJ_TM, LRU_CW), lru_col),
            pl.BlockSpec((INPROJ_TM, LRU_CW), lru_col),
            pl.BlockSpec((INPROJ_TM, LRU_CW), lru_col),
            pl.BlockSpec((SUBLANES, LRU_CW), lru_col),
            pl.BlockSpec((SUBLANES, LRU_CW), lru_col),
            pl.BlockSpec((SUBLANES, LRU_CW), lru_col),
            pl.BlockSpec((SUBLANES, LRU_CW), lru_col),
        ),
        scratch_shapes=[
            pltpu.VMEM((INPROJ_TM, D_MODEL), F32),
            pltpu.SemaphoreType.DMA(()),
            pltpu.VMEM((XN_ROWS, D_MODEL), BF16),
            pltpu.VMEM((LRU_BLOCKS, U_ROWS, LANES), F32),
            pltpu.VMEM((LRU_BLOCKS, SUBLANES, LANES), F32),
            scan_scratch,
            pltpu.VMEM((LRU_NK, INPROJ_TM, LANES), BF16),
            scan_scratch, scan_scratch, scan_scratch, scan_scratch,
        ],
        compiler_params=pltpu.CompilerParams(
            dimension_semantics=("arbitrary", "arbitrary"),
            vmem_limit_bytes=VMEM_LIMIT),
        name="inproj",
    )(x2, x2, norm_w, w_in_bf16, *rope_inputs, lru_params, w_gates)


def _lru_carry_kernel(ef_ref, pef_ref, eb_ref, peb_ref, cf_ref, cbk_ref):
    n_sub = ef_ref.shape[0]

    def fwd(g, carry):
        row = pl.ds(g, 1)
        cf_ref[row, :] = carry
        return ef_ref[row, :] + pef_ref[row, :] * carry

    def bwd(i, carry):
        row = pl.ds(n_sub - 1 - i, 1)
        cbk_ref[row, :] = carry
        return eb_ref[row, :] + peb_ref[row, :] * carry

    zero = jnp.zeros((1, ef_ref.shape[1]), F32)
    lax.fori_loop(0, n_sub, fwd, zero)
    lax.fori_loop(0, n_sub, bwd, zero)


def _lru_carry(ef, pef, eb, peb):
    shape = jax.ShapeDtypeStruct(ef.shape, F32)
    return pl.pallas_call(
        _lru_carry_kernel,
        out_shape=(shape, shape),
        name="lru_carry",
    )(ef, pef, eb, peb)


ATT_QB = 4
ATT_TQ = ATT_QB * BLOCK
assert WINDOW == BLOCK


def _attn_kernel(n_steps, sink_ref, q_ref, kl_ref, kc_ref, kr_ref, vl_ref, vc_ref, vr_ref,
                 sg_ref, o_ref):
    step = pl.program_id(0)
    k_ext = jnp.concatenate([kl_ref[...], kc_ref[...], kr_ref[...]], axis=0)
    v_ext = jnp.concatenate([vl_ref[...], vc_ref[...], vr_ref[...]], axis=0)

    q_idx = lax.broadcasted_iota(jnp.int32, (BLOCK, BLOCK), 0)
    s_idx = lax.broadcasted_iota(jnp.int32, (BLOCK, BLOCK), 1)
    tri_left = jnp.where(s_idx >= q_idx, 0.0, MASK_VALUE)
    tri_right = jnp.where(s_idx <= q_idx, 0.0, MASK_VALUE)

    for kh in range(N_KV_HEADS):
        kv_cols = slice(kh * HEAD_DIM, (kh + 1) * HEAD_DIM)
        sinks = [sink_ref[kh * GROUP + g] * LOG2E for g in range(GROUP)]
        for qb in range(ATT_QB):
            bias_l, bias_r = tri_left, tri_right
            if qb == 0:
                bias_l = jnp.where(step > 0, tri_left, MASK_VALUE)
            if qb == ATT_QB - 1:
                bias_r = jnp.where(step < n_steps - 1, tri_right, MASK_VALUE)
            q_rows = slice(qb * BLOCK, (qb + 1) * BLOCK)
            k_rows = slice(qb * BLOCK, (qb + 3) * BLOCK)
            q = jnp.concatenate(
                [q_ref[q_rows, (kh * GROUP + g) * HEAD_DIM:(kh * GROUP + g + 1) * HEAD_DIM]
                 for g in range(GROUP)], axis=0)
            scores = lax.dot_general(q, k_ext[k_rows, kv_cols], (((1,), (1,)), ((), ())),
                                     preferred_element_type=F32)
            probs, denoms = [], []
            for g in range(GROUP):
                head = scores[g * BLOCK:(g + 1) * BLOCK, :]
                s_l = head[:, :BLOCK] + bias_l
                s_c = head[:, BLOCK:2 * BLOCK]
                s_r = head[:, 2 * BLOCK:] + bias_r
                m = jnp.max(jnp.maximum(jnp.maximum(s_l, s_c), s_r), axis=-1, keepdims=True)
                m = jnp.maximum(m, sinks[g])
                e_l, e_c, e_r = jnp.exp2(s_l - m), jnp.exp2(s_c - m), jnp.exp2(s_r - m)
                denoms.append(jnp.sum(e_l + e_c + e_r, axis=-1, keepdims=True) + jnp.exp2(sinks[g] - m))
                probs.append(jnp.concatenate(
                    [e_l.astype(BF16), e_c.astype(BF16), e_r.astype(BF16)], axis=1))
            out = jnp.dot(jnp.concatenate(probs, axis=0), v_ext[k_rows, kv_cols],
                          preferred_element_type=F32)
            for g in range(GROUP):
                cols = slice((kh * GROUP + g) * HEAD_DIM, (kh * GROUP + g + 1) * HEAD_DIM)
                gate = sg_ref[q_rows, cols].astype(F32)
                o_ref[q_rows, cols] = (out[g * BLOCK:(g + 1) * BLOCK, :] / denoms[g] * gate
                                       ).astype(o_ref.dtype)


def _attention(zb, sink):
    s = zb.shape[0]
    n_steps = s // ATT_TQ
    n_blocks = s // BLOCK
    aw = N_HEADS * HEAD_DIM

    def left(t):
        return jnp.maximum(t * ATT_QB - 1, 0)

    def right(t):
        return jnp.minimum((t + 1) * ATT_QB, n_blocks - 1)

    kcol = B_K // KV_WIDTH
    vcol = B_V // KV_WIDTH
    in_specs = [
        pl.BlockSpec(memory_space=pltpu.SMEM),
        pl.BlockSpec((ATT_TQ, aw), lambda t: (t, B_Q // aw)),
        pl.BlockSpec((BLOCK, KV_WIDTH), lambda t: (left(t), kcol)),
        pl.BlockSpec((ATT_TQ, KV_WIDTH), lambda t: (t, kcol)),
        pl.BlockSpec((BLOCK, KV_WIDTH), lambda t: (right(t), kcol)),
        pl.BlockSpec((BLOCK, KV_WIDTH), lambda t: (left(t), vcol)),
        pl.BlockSpec((ATT_TQ, KV_WIDTH), lambda t: (t, vcol)),
        pl.BlockSpec((BLOCK, KV_WIDTH), lambda t: (right(t), vcol)),
        pl.BlockSpec((ATT_TQ, aw), lambda t: (t, B_SG_ATT // aw)),
    ]
    return pl.pallas_call(
        functools.partial(_attn_kernel, n_steps),
        out_shape=jax.ShapeDtypeStruct((s, aw), BF16),
        grid=(n_steps,),
        in_specs=in_specs,
        out_specs=pl.BlockSpec((ATT_TQ, aw), lambda t: (t, 0)),
        compiler_params=pltpu.CompilerParams(
            dimension_semantics=("parallel",),
            vmem_limit_bytes=VMEM_LIMIT),
        name="attn",
    )(sink, zb, zb, zb, zb, zb, zb, zb, zb)


MERGE_TM = 256
MERGE_TN = 512


def _merge_kernel(h_ref, pf_ref, pb_ref, sg_ref, yb_ref, ga_ref, gb_ref, cf_ref, cbk_ref,
                  wa_ref, wb_ref, o_ref):
    i = pl.program_id(0)

    def branch_b(cols):
        return jnp.dot(yb_ref[...], wb_ref[:, cols], preferred_element_type=F32)

    first = slice(0, MERGE_TN)
    pb = branch_b(first)
    parts = []
    for q in range(MERGE_TM // LRU_SUB):
        g = i * (MERGE_TM // LRU_SUB) + q
        rows = slice(q * LRU_SUB, (q + 1) * LRU_SUB)
        h = (h_ref[rows, :]
             + pf_ref[rows, :].astype(F32) * cf_ref[pl.ds(g, 1), :]
             + pb_ref[rows, :].astype(F32) * cbk_ref[pl.ds(g, 1), :])
        parts.append((h * sg_ref[rows, :].astype(F32)).astype(BF16))
    y_a = jnp.concatenate(parts, axis=0)
    for c0 in range(0, D_MODEL, MERGE_TN):
        cols = slice(c0, c0 + MERGE_TN)
        if c0 > 0:
            pb = branch_b(cols)
        pa = jnp.dot(y_a, wa_ref[:, cols], preferred_element_type=F32)
        o_ref[:, cols] = (ga_ref[:, cols].astype(F32) * pa
                          + gb_ref[:, cols].astype(F32) * pb).astype(BF16)


def _merge(h_loc, p_f, p_b, y_b, zb, c_f, c_b, wa, wb):
    s = h_loc.shape[0]
    row = lambda i: (i, 0)
    const = lambda i: (0, 0)
    resident = functools.partial(pl.BlockSpec, pipeline_mode=pl.Buffered(1))
    tile = (MERGE_TM, D_MODEL)
    return pl.pallas_call(
        _merge_kernel,
        out_shape=jax.ShapeDtypeStruct((s, D_MODEL), BF16),
        grid=(s // MERGE_TM,),
        in_specs=[
            pl.BlockSpec(tile, row),
            pl.BlockSpec(tile, row),
            pl.BlockSpec(tile, row),
            pl.BlockSpec(tile, lambda i: (i, B_SG_LRU // D_MODEL)),
            pl.BlockSpec(tile, row),
            pl.BlockSpec(tile, lambda i: (i, B_GM_LRU // D_MODEL)),
            pl.BlockSpec(tile, lambda i: (i, B_GM_ATT // D_MODEL)),
            resident(c_f.shape, const),
            resident(c_b.shape, const),
            resident((D_MODEL, D_MODEL), const),
            resident((D_MODEL, D_MODEL), const),
        ],
        out_specs=pl.BlockSpec(tile, row),
        compiler_params=pltpu.CompilerParams(
            dimension_semantics=("parallel",),
            vmem_limit_bytes=VMEM_LIMIT),
        name="merge",
    )(h_loc, p_f, p_b, zb, y_b, zb, zb, c_f, c_b, wa, wb)


OUT_TM = 512


def _out_kernel(m_ref, x_ref, wo_ref, nw_ref, o_ref):
    y = jnp.dot(m_ref[...], wo_ref[...], preferred_element_type=F32)
    ms = jnp.mean(y * y, axis=-1, keepdims=True)
    o_ref[...] = x_ref[...] + y * lax.rsqrt(ms + NORM_EPS) * nw_ref[...]


def _out_proj(merged, x2, wo, norm_w):
    s = x2.shape[0]
    row = lambda i: (i, 0)
    const = lambda i: (0, 0)
    return pl.pallas_call(
        _out_kernel,
        out_shape=jax.ShapeDtypeStruct((s, D_MODEL), F32),
        grid=(s // OUT_TM,),
        in_specs=[
            pl.BlockSpec((OUT_TM, D_MODEL), row),
            pl.BlockSpec((OUT_TM, D_MODEL), row),
            pl.BlockSpec((D_MODEL, D_MODEL), const, pipeline_mode=pl.Buffered(1)),
            pl.BlockSpec((1, D_MODEL), const),
        ],
        out_specs=pl.BlockSpec((OUT_TM, D_MODEL), row),
        compiler_params=pltpu.CompilerParams(
            dimension_semantics=("parallel",),
            vmem_limit_bytes=VMEM_LIMIT),
        name="out_proj",
    )(merged, x2, wo, norm_w)


def _rope_inputs():
    inv_freq = ROPE_THETA ** (-jnp.arange(0, ROT_DIM, 2, dtype=F32) / ROT_DIM)
    freq = jnp.concatenate([inv_freq, inv_freq, jnp.zeros((LANES - ROT_DIM,), F32)])[None, :]
    ang = jnp.arange(INPROJ_TM, dtype=F32)[:, None] * freq
    return freq, jnp.cos(ang), jnp.sin(ang)


def _gate_weights(w_r, w_i):
    w = jnp.concatenate([w_r[0], w_i[0], w_r[1], w_i[1]], axis=-1)
    return (0.5 * w).astype(BF16)


def _lru_params(conv_w, conv_b, b_r, b_i, lam):
    rows = jnp.concatenate([
        conv_w, conv_b[None, :],
        0.5 * jnp.stack([b_r[0], b_i[0], b_r[1], b_i[1]]),
        lam,
        jnp.zeros((P_ROWS - P_LAMBDA - 2, D_MODEL), F32)], axis=0)
    return rows.reshape(P_ROWS, LRU_BLOCKS, LANES).transpose(1, 0, 2)


def kernel(x, norm_pre_w, w_in, conv_w, conv_b, lru_w_r, lru_b_r, lru_w_i, lru_b_i,
           lru_lambda, attn_sink, w_proj_a, w_proj_b, w_out, norm_post_w):
    bsz, s, d = x.shape
    depth = w_in.shape[0]
    rope_inputs = _rope_inputs()
    outs = []
    for bi in range(bsz):
        xb = x[bi]
        for l in range(depth):
            zb, h_loc, p_f, p_b, ef, pef, eb, peb = _inproj(
                xb, norm_pre_w[l][None, :], w_in[l].astype(BF16), rope_inputs,
                _lru_params(conv_w[l], conv_b[l], lru_b_r[l], lru_b_i[l], lru_lambda[l]),
                _gate_weights(lru_w_r[l], lru_w_i[l]))
            c_f, c_b = _lru_carry(ef, pef, eb, peb)
            y_b = _attention(zb, attn_sink[l])
            merged = _merge(h_loc, p_f, p_b, y_b, zb, c_f, c_b,
                            w_proj_a[l].astype(BF16), w_proj_b[l].astype(BF16))
            xb = _out_proj(merged, xb, w_out[l].astype(BF16), norm_post_w[l][None, :])
        outs.append(xb)
    return jnp.stack(outs, axis=0)
```

```python
import functools
import math

import jax
import jax.numpy as jnp
from jax import lax
from jax.experimental import pallas as pl
from jax.experimental.pallas import tpu as pltpu

F32 = jnp.float32
BF16 = jnp.bfloat16

D_MODEL = 2048
LRU_BLOCK_DIM = 128
LRU_BLOCKS = D_MODEL // LRU_BLOCK_DIM
LRU_C = 8.0
N_HEADS = 16
N_KV_HEADS = 4
GROUP = N_HEADS // N_KV_HEADS
HEAD_DIM = 128
KV_WIDTH = N_KV_HEADS * HEAD_DIM
BLOCK = 128
WINDOW = 128
ROPE_THETA = 500000.0
ROT_DIM = HEAD_DIM // 4
NORM_EPS = 1e-6
MASK_VALUE = -1e30
LOG2E = math.log2(math.e)

B_SG_LRU, B_SG_ATT, B_GM_LRU, B_GM_ATT, B_Q, B_K, B_V = (
    0, 2048, 4096, 6144, 8192, 10240, 10752)
B_WIDTH = 11264

SUBLANES = 8
BF16_SUBLANES = 16
LANES = 128
VMEM_LIMIT = 56 * 1024 * 1024


def _sigmoid(x):
    return 0.5 * jnp.tanh(0.5 * x) + 0.5


def _silu(x):
    hx = 0.5 * x
    return hx * jnp.tanh(hx) + hx


def _sqrt_nonneg(x):
    return jnp.where(x > 0.0, x * lax.rsqrt(x), 0.0)


INPROJ_TM = 1024
INPROJ_TN = 1024
ROPE_ROWS = 256
NORM_ROWS = 64
XN_ROWS = INPROJ_TM + BF16_SUBLANES
J_U, J_SILU, J_SIG, J_Q, J_KV, J_END = 0, 2, 6, 10, 12, 13

LRU_SUB = 128
LRU_CW = 256
LRU_NK = LRU_CW // LRU_BLOCK_DIM
LRU_STEPS = D_MODEL // LRU_CW
SLABS_PER_J = INPROJ_TN // LANES
CONV_LEFT = 2
U_ROWS = (LRU_SUB + 3) * SUBLANES
CONV_TILE = 128
COEFF_TILE = 64
GATE_COLS = 4 * LRU_BLOCK_DIM
P_CONV_W, P_CONV_B, P_BIAS, P_LAMBDA, P_ROWS = 0, 4, 5, 9, 16


def _weight_block(j):
    return jnp.where(j < 4, j, jnp.where(j < 10, j + 3, j - 6))


def _rope_tables(freq_ref, rc_ref, rs_ref, chunk, scale):
    lane = lax.broadcasted_iota(jnp.int32, (1, LANES), 1)
    base = (chunk * INPROJ_TM).astype(F32) * freq_ref[...]
    cb, sb = jnp.cos(base) * scale, jnp.sin(base) * scale
    sign = jnp.where(lane < ROT_DIM // 2, -1.0, jnp.where(lane < ROT_DIM, 1.0, 0.0))
    rc, rs = rc_ref[...], rs_ref[...]
    cos_t = cb * rc - sb * rs
    sin_t = (sign * sb) * rc + (sign * cb) * rs
    return cos_t, sin_t


def _rope(t, cos_t, sin_t):
    half = ROT_DIM // 2
    lane = lax.broadcasted_iota(jnp.int32, t.shape, 1)
    partner = jnp.where(lane < half,
                        pltpu.roll(t, HEAD_DIM - half, axis=1),
                        pltpu.roll(t, half, axis=1))
    return t * cos_t + partner * sin_t


def _lru_conv(kp, lp_ref, u_s, uc_s, ucb_s):
    for k in range(LRU_NK):
        kk = kp * LRU_NK + k
        par = lp_ref.at[kk]
        u_k = u_s.at[kk]
        for t0 in range(0, INPROJ_TM, CONV_TILE):
            y = par[P_CONV_B:P_CONV_B + 1, :]
            for tap in range(4):
                lo = t0 + tap * SUBLANES
                y = y + u_k[lo:lo + CONV_TILE, :] * par[P_CONV_W + tap:P_CONV_W + tap + 1, :]
            uc_s[k, t0:t0 + CONV_TILE, :] = y
            ucb_s[k, t0:t0 + CONV_TILE, :] = y.astype(BF16)


def _lru_gates(kp, wg_ref, ucb_s, z_s):
    for k in range(LRU_NK):
        z = jnp.dot(ucb_s[k], wg_ref[kp * LRU_NK + k], preferred_element_type=F32)
        for c, zc_s in enumerate(z_s):
            zc_s[k] = z[:, c * LANES:(c + 1) * LANES]


def _zero_after(x):
    bits = lax.shift_right_logical(pltpu.bitcast(x, jnp.uint32), jnp.uint32(16))
    return pltpu.bitcast(lax.shift_right_logical(bits, jnp.uint32(16)), F32)


def _lru_coeffs(kp, lp_ref, uc_s, ab_s, pace):
    tiles_per_col = (INPROJ_TM // 2) // COEFF_TILE
    for k in range(LRU_NK):
        par = lp_ref.at[kp * LRU_NK + k]
        for d in range(2):
            a_s, b_s = ab_s[2 * d], ab_s[2 * d + 1]
            pace_half = pace[k]
            nl = -par[P_LAMBDA + d:P_LAMBDA + d + 1, :]
            softplus = jnp.maximum(nl, 0.0) + jnp.log1p(jnp.exp(-jnp.abs(nl)))
            dq = -0.25 * LRU_C * softplus
            br_h = par[P_BIAS + 2 * d:P_BIAS + 2 * d + 1, :]
            bi_h = par[P_BIAS + 2 * d + 1:P_BIAS + 2 * d + 2, :]
            for t0 in range(0, INPROJ_TM, COEFF_TILE):
                rows = slice(t0, t0 + COEFF_TILE)
                t = t0 // COEFF_TILE
                p_rows = slice((t % tiles_per_col) * COEFF_TILE, (t % tiles_per_col + 1) * COEFF_TILE)
                p_col = (d * 2 + t // tiles_per_col) * LANES
                zero = _zero_after(pace_half[p_rows, p_col:p_col + LANES])
                tr = jnp.tanh(a_s[k, rows, :] + br_h + zero)
                th = jnp.tanh(tr * dq + dq)
                inv = 1.0 / (1.0 - th)
                root = _sqrt_nonneg(-th)
                ti = jnp.tanh(b_s[k, rows, :] + bi_h)
                a_s[k, rows, :] = (1.0 + th) * inv
                b_s[k, rows, :] = (root * inv) * ((ti + 1.0) * uc_s[k, rows, :])


def _lru_scan(ab_s, h_ref, pf_ref, pb_ref, ef_ref, pef_ref, eb_ref, peb_ref):
    zif_s, zrf_s, zib_s, zrb_s = ab_s
    half = LRU_SUB // 2

    def step(k, d, r, h, p):
        rows = slice(r * SUBLANES, (r + 1) * SUBLANES)
        zr_s, zi_s = (zrf_s, zif_s) if d == 0 else (zrb_s, zib_s)
        a = zi_s[k, rows, :]
        h = a * h + zr_s[k, rows, :]
        p = a * p
        crossed = (r >= half) if d == 0 else (r < half)
        if crossed:
            other = zrb_s if d == 0 else zrf_s
            zrf_s[k, rows, :] = h + other[k, rows, :]
        else:
            zr_s[k, rows, :] = h
        zi_s[k, rows, :] = p
        return h, p

    zero = jnp.zeros((SUBLANES, LANES), F32)
    one = jnp.ones((SUBLANES, LANES), F32)
    state = [[zero, one, zero, one] for _ in range(LRU_NK)]
    for it in range(LRU_SUB):
        for k in range(LRU_NK):
            hf, pf, hb, pb = state[k]
            hf, pf = step(k, 0, it, hf, pf)
            hb, pb = step(k, 1, LRU_SUB - 1 - it, hb, pb)
            state[k] = [hf, pf, hb, pb]

    for k in range(LRU_NK):
        cols = slice(k * LANES, (k + 1) * LANES)
        hf, pf, hb, pb = state[k]
        ef_ref[:, cols] = hf
        pef_ref[:, cols] = pf
        eb_ref[:, cols] = hb
        peb_ref[:, cols] = pb
        for s in range(SUBLANES):
            rows = slice(s * LRU_SUB, (s + 1) * LRU_SUB)
            sel = pl.ds(s, LRU_SUB, stride=SUBLANES)
            h_ref[rows, cols] = zrf_s[k, sel, :]
            pf_ref[rows, cols] = zif_s[k, sel, :].astype(BF16)
            pb_ref[rows, cols] = zib_s[k, sel, :].astype(BF16)


def _inproj_kernel(n_chunks, x_ref, xh_ref, nw_ref, w_ref, freq_ref, rc_ref, rs_ref, lp_ref, wg_ref,
                   b_ref, h_ref, pf_ref, pb_ref, ef_ref, pef_ref, eb_ref, peb_ref,
                   x_buf, x_sem, xn_ref, u_s, uprev_s, uc_s, ucb_s, zrf_s, zif_s, zrb_s, zib_s):
    i = pl.program_id(0)
    j = pl.program_id(1)
    z_s = (zrf_s, zif_s, zrb_s, zib_s)
    lru_outs = (h_ref, pf_ref, pb_ref, ef_ref, pef_ref, eb_ref, peb_ref)

    def x_copy(chunk):
        rows = pl.ds(pl.multiple_of(chunk * INPROJ_TM, INPROJ_TM), INPROJ_TM)
        return pltpu.make_async_copy(x_ref.at[rows, :], x_buf, x_sem)

    @pl.when((i == 0) & (j == 0))
    def _():
        x_copy(0).start()

    @pl.when((j == 1) & (i < n_chunks - 1))
    def _():
        x_copy(i + 1).start()

    def project(lo=0, hi=INPROJ_TM):
        return jnp.dot(xn_ref[lo:hi, :], w_ref[...], preferred_element_type=F32)

    def project_with_lru(epilogue):
        kp = j - J_SILU
        _lru_gates(kp, wg_ref, ucb_s, z_s)
        half = INPROJ_TM // 2
        acc = [project(0, half), project(half, INPROJ_TM)]
        b_ref[0:half, :] = epilogue(acc[0]).astype(BF16)
        b_ref[half:, :] = epilogue(acc[1]).astype(BF16)
        _lru_coeffs(kp, lp_ref, uc_s, z_s, acc)
        _lru_scan(z_s, *lru_outs)
        _lru_conv(jnp.minimum(kp + 1, LRU_STEPS - 1), lp_ref, u_s, uc_s, ucb_s)

    def normalise(xv):
        ms = jnp.mean(xv * xv, axis=-1, keepdims=True)
        return (xv * lax.rsqrt(ms + NORM_EPS) * nw_ref[...]).astype(BF16)

    def u_step(with_norm):
        sub = lax.broadcasted_iota(jnp.int32, (SUBLANES, LANES), 0)
        first_slab = CONV_LEFT * SUBLANES
        for lo in range(0, INPROJ_TM, ROPE_ROWS):
            last = lo + ROPE_ROWS == INPROJ_TM
            if with_norm:
                for r0 in range(lo, lo + ROPE_ROWS, NORM_ROWS):
                    xn_ref[r0:r0 + NORM_ROWS, :] = normalise(x_buf[r0:r0 + NORM_ROWS, :])
                if last:
                    xn_ref[INPROJ_TM:, :] = normalise(xh_ref[...])
            acc = project(lo, XN_ROWS if last else lo + ROPE_ROWS)
            for k in range(SLABS_PER_J):
                u_k = u_s.at[j * SLABS_PER_J + k]
                for s in range(lo // LRU_SUB, (lo + ROPE_ROWS) // LRU_SUB):
                    u_k[pl.ds(first_slab + s, LRU_SUB, stride=SUBLANES), :] = (
                        acc[s * LRU_SUB - lo:(s + 1) * LRU_SUB - lo, k * LANES:(k + 1) * LANES])
        tail = INPROJ_TM - lo
        for k in range(SLABS_PER_J):
            kk = j * SLABS_PER_J + k
            cols = slice(k * LANES, (k + 1) * LANES)
            u_k = u_s.at[kk]
            prev = uprev_s[kk]
            prev1 = jnp.where(i > 0, prev[SUBLANES - 1:SUBLANES, :], 0.0)
            prev2 = jnp.where(i > 0, prev[SUBLANES - 2:SUBLANES - 1, :], 0.0)
            nxt = jnp.where(i < n_chunks - 1, acc[tail:tail + 1, cols], 0.0)
            last1 = u_k[pl.ds(first_slab + (LRU_SUB - 1) * SUBLANES, SUBLANES), :]
            last2 = u_k[pl.ds(first_slab + (LRU_SUB - 2) * SUBLANES, SUBLANES), :]
            head = u_k[pl.ds(first_slab, SUBLANES), :]
            u_k[pl.ds(SUBLANES, SUBLANES), :] = jnp.where(sub == 0, prev1, pltpu.roll(last1, 1, axis=0))
            u_k[pl.ds(0, SUBLANES), :] = jnp.where(sub == 0, prev2, pltpu.roll(last2, 1, axis=0))
            u_k[pl.ds(first_slab + LRU_SUB * SUBLANES, SUBLANES), :] = jnp.where(
                sub == SUBLANES - 1, nxt, pltpu.roll(head, SUBLANES - 1, axis=0))
            uprev_s[kk] = acc[tail - SUBLANES:tail, cols]
        if not with_norm:
            _lru_conv(0, lp_ref, u_s, uc_s, ucb_s)

    @pl.when(j == 0)
    def _():
        x_copy(i).wait()
        u_step(with_norm=True)

    @pl.when(j == 1)
    def _():
        u_step(with_norm=False)

    @pl.when((j >= J_SILU) & (j < J_SIG))
    def _():
        project_with_lru(_silu)

    @pl.when((j >= J_SIG) & (j < J_Q))
    def _():
        project_with_lru(_sigmoid)

    @pl.when((j >= J_Q) & (j < J_KV))
    def _():
        cos_t, sin_t = _rope_tables(freq_ref, rc_ref, rs_ref, i, HEAD_DIM ** -0.5 * LOG2E)
        for lo in range(0, INPROJ_TM, ROPE_ROWS):
            rows = slice(lo, lo + ROPE_ROWS)
            acc = project(lo, lo + ROPE_ROWS)
            for hb in range(INPROJ_TN // HEAD_DIM):
                cols = slice(hb * HEAD_DIM, (hb + 1) * HEAD_DIM)
                b_ref[rows, cols] = _rope(acc[:, cols], cos_t[rows], sin_t[rows]).astype(BF16)

    @pl.when(j == J_KV)
    def _():
        cos_t, sin_t = _rope_tables(freq_ref, rc_ref, rs_ref, i, 1.0)
        for lo in range(0, INPROJ_TM, ROPE_ROWS):
            rows = slice(lo, lo + ROPE_ROWS)
            acc = project(lo, lo + ROPE_ROWS)
            for hb in range(N_KV_HEADS):
                cols = slice(hb * HEAD_DIM, (hb + 1) * HEAD_DIM)
                b_ref[rows, cols] = _rope(acc[:, cols], cos_t[rows], sin_t[rows]).astype(BF16)
            b_ref[rows, KV_WIDTH:] = acc[:, KV_WIDTH:].astype(BF16)


def _inproj(x2, norm_w, w_in_bf16, rope_inputs, lru_params, w_gates):
    s = x2.shape[0]
    n_chunks = s // INPROJ_TM
    n_sub = s // LRU_SUB
    halo_per_chunk = INPROJ_TM // BF16_SUBLANES
    n_halo = s // BF16_SUBLANES
    lru_col = lambda i, j: (i, jnp.clip(j - J_SILU, 0, LRU_STEPS - 1))
    const2 = lambda i, j: (0, 0)
    const3 = lambda i, j: (0, 0, 0)
    scan_scratch = pltpu.VMEM((LRU_NK, INPROJ_TM, LANES), F32)
    return pl.pallas_call(
        functools.partial(_inproj_kernel, n_chunks),
        out_shape=(jax.ShapeDtypeStruct((s, B_WIDTH), BF16),
                   jax.ShapeDtypeStruct((s, D_MODEL), F32),
                   jax.ShapeDtypeStruct((s, D_MODEL), BF16),
                   jax.ShapeDtypeStruct((s, D_MODEL), BF16),
                   jax.ShapeDtypeStruct((n_sub, D_MODEL), F32),
                   jax.ShapeDtypeStruct((n_sub, D_MODEL), F32),
                   jax.ShapeDtypeStruct((n_sub, D_MODEL), F32),
                   jax.ShapeDtypeStruct((n_sub, D_MODEL), F32)),
        grid=(n_chunks, J_END),
        in_specs=[
            pl.BlockSpec(memory_space=pl.ANY),
            pl.BlockSpec((BF16_SUBLANES, D_MODEL),
                         lambda i, j: (jnp.minimum((i + 1) * halo_per_chunk, n_halo - 1), 0)),
            pl.BlockSpec((1, D_MODEL), const2),
            pl.BlockSpec((D_MODEL, INPROJ_TN), lambda i, j: (0, _weight_block(j))),
            pl.BlockSpec((1, LANES), const2),
            pl.BlockSpec((INPROJ_TM, LANES), const2, pipeline_mode=pl.Buffered(1)),
            pl.BlockSpec((INPROJ_TM, LANES), const2, pipeline_mode=pl.Buffered(1)),
            pl.BlockSpec((LRU_BLOCKS, P_ROWS, LANES), const3),
            pl.BlockSpec((LRU_BLOCKS, LRU_BLOCK_DIM, GATE_COLS), const3),
        ],
        out_specs=(
            pl.BlockSpec((INPROJ_TM, INPROJ_TN), lambda i, j: (i, jnp.maximum(j - J_SILU, 0))),
            pl.BlockSpec((INPROJ_TM, LRU_CW), lru_col),
            pl.BlockSpec((INPROJ_TM, LRU_CW), lru_col),
            pl.BlockSpec((INPROJ_TM, LRU_CW), lru_col),
            pl.BlockSpec((SUBLANES, LRU_CW), lru_col),
            pl.BlockSpec((SUBLANES, LRU_CW), lru_col),
            pl.BlockSpec((SUBLANES, LRU_CW), lru_col),
            pl.BlockSpec((SUBLANES, LRU_CW), lru_col),
        ),
        scratch_shapes=[
            pltpu.VMEM((INPROJ_TM, D_MODEL), F32),
            pltpu.SemaphoreType.DMA(()),
            pltpu.VMEM((XN_ROWS, D_MODEL), BF16),
            pltpu.VMEM((LRU_BLOCKS, U_ROWS, LANES), F32),
            pltpu.VMEM((LRU_BLOCKS, SUBLANES, LANES), F32),
            scan_scratch,
            pltpu.VMEM((LRU_NK, INPROJ_TM, LANES), BF16),
            scan_scratch, scan_scratch, scan_scratch, scan_scratch,
        ],
        compiler_params=pltpu.CompilerParams(
            dimension_semantics=("arbitrary", "arbitrary"),
            vmem_limit_bytes=VMEM_LIMIT),
        name="inproj",
    )(x2, x2, norm_w, w_in_bf16, *rope_inputs, lru_params, w_gates)


def _lru_carry_kernel(ef_ref, pef_ref, eb_ref, peb_ref, cf_ref, cbk_ref):
    n_sub = ef_ref.shape[0]

    def fwd(g, carry):
        row = pl.ds(g, 1)
        cf_ref[row, :] = carry
        return ef_ref[row, :] + pef_ref[row, :] * carry

    def bwd(i, carry):
        row = pl.ds(n_sub - 1 - i, 1)
        cbk_ref[row, :] = carry
        return eb_ref[row, :] + peb_ref[row, :] * carry

    zero = jnp.zeros((1, ef_ref.shape[1]), F32)
    lax.fori_loop(0, n_sub, fwd, zero)
    lax.fori_loop(0, n_sub, bwd, zero)


def _lru_carry(ef, pef, eb, peb):
    shape = jax.ShapeDtypeStruct(ef.shape, F32)
    return pl.pallas_call(
        _lru_carry_kernel,
        out_shape=(shape, shape),
        name="lru_carry",
    )(ef, pef, eb, peb)


ATT_QB = 4
ATT_TQ = ATT_QB * BLOCK
assert WINDOW == BLOCK


def _attn_kernel(n_steps, sink_ref, q_ref, kl_ref, kc_ref, kr_ref, vl_ref, vc_ref, vr_ref,
                 sg_ref, o_ref):
    step = pl.program_id(0)
    k_ext = jnp.concatenate([kl_ref[...], kc_ref[...], kr_ref[...]], axis=0)
    v_ext = jnp.concatenate([vl_ref[...], vc_ref[...], vr_ref[...]], axis=0)

    q_idx = lax.broadcasted_iota(jnp.int32, (BLOCK, BLOCK), 0)
    s_idx = lax.broadcasted_iota(jnp.int32, (BLOCK, BLOCK), 1)
    tri_left = jnp.where(s_idx >= q_idx, 0.0, MASK_VALUE)
    tri_right = jnp.where(s_idx <= q_idx, 0.0, MASK_VALUE)

    def unit_scores(kh, qb):
        q = jnp.concatenate(
            [q_ref[qb * BLOCK:(qb + 1) * BLOCK, (kh * GROUP + g) * HEAD_DIM:(kh * GROUP + g + 1) * HEAD_DIM]
             for g in range(GROUP)], axis=0)
        return lax.dot_general(q, k_ext[qb * BLOCK:(qb + 3) * BLOCK, kh * HEAD_DIM:(kh + 1) * HEAD_DIM],
                               (((1,), (1,)), ((), ())),
                               preferred_element_type=F32)

    units = [(kh, qb) for kh in range(N_KV_HEADS) for qb in range(ATT_QB)]
    next_scores = unit_scores(*units[0])
    for u, (kh, qb) in enumerate(units):
        scores = next_scores
        if u + 1 < len(units):
            next_scores = unit_scores(*units[u + 1])
        kv_cols = slice(kh * HEAD_DIM, (kh + 1) * HEAD_DIM)
        sinks = [sink_ref[kh * GROUP + g] * LOG2E for g in range(GROUP)]
        bias_l, bias_r = tri_left, tri_right
        if qb == 0:
            bias_l = jnp.where(step > 0, tri_left, MASK_VALUE)
        if qb == ATT_QB - 1:
            bias_r = jnp.where(step < n_steps - 1, tri_right, MASK_VALUE)
        q_rows = slice(qb * BLOCK, (qb + 1) * BLOCK)
        k_rows = slice(qb * BLOCK, (qb + 3) * BLOCK)
        probs, denoms = [], []
        for g in range(GROUP):
            head = scores[g * BLOCK:(g + 1) * BLOCK, :]
            s_l = head[:, :BLOCK] + bias_l
            s_c = head[:, BLOCK:2 * BLOCK]
            s_r = head[:, 2 * BLOCK:] + bias_r
            m = jnp.max(jnp.maximum(jnp.maximum(s_l, s_c), s_r), axis=-1, keepdims=True)
            m = jnp.maximum(m, sinks[g])
            e_l, e_c, e_r = jnp.exp2(s_l - m), jnp.exp2(s_c - m), jnp.exp2(s_r - m)
            denoms.append(jnp.sum(e_l + e_c + e_r, axis=-1, keepdims=True) + jnp.exp2(sinks[g] - m))
            probs.append(jnp.concatenate(
                [e_l.astype(BF16), e_c.astype(BF16), e_r.astype(BF16)], axis=1))
        out = jnp.dot(jnp.concatenate(probs, axis=0), v_ext[k_rows, kv_cols],
                      preferred_element_type=F32)
        for g in range(GROUP):
            cols = slice((kh * GROUP + g) * HEAD_DIM, (kh * GROUP + g + 1) * HEAD_DIM)
            gate = sg_ref[q_rows, cols].astype(F32)
            o_ref[q_rows, cols] = (out[g * BLOCK:(g + 1) * BLOCK, :] / denoms[g] * gate
                                   ).astype(o_ref.dtype)


def _attention(zb, sink):
    s = zb.shape[0]
    n_steps = s // ATT_TQ
    n_blocks = s // BLOCK
    aw = N_HEADS * HEAD_DIM

    def left(t):
        return jnp.maximum(t * ATT_QB - 1, 0)

    def right(t):
        return jnp.minimum((t + 1) * ATT_QB, n_blocks - 1)

    kcol = B_K // KV_WIDTH
    vcol = B_V // KV_WIDTH
    in_specs = [
        pl.BlockSpec(memory_space=pltpu.SMEM),
        pl.BlockSpec((ATT_TQ, aw), lambda t: (t, B_Q // aw)),
        pl.BlockSpec((BLOCK, KV_WIDTH), lambda t: (left(t), kcol)),
        pl.BlockSpec((ATT_TQ, KV_WIDTH), lambda t: (t, kcol)),
        pl.BlockSpec((BLOCK, KV_WIDTH), lambda t: (right(t), kcol)),
        pl.BlockSpec((BLOCK, KV_WIDTH), lambda t: (left(t), vcol)),
        pl.BlockSpec((ATT_TQ, KV_WIDTH), lambda t: (t, vcol)),
        pl.BlockSpec((BLOCK, KV_WIDTH), lambda t: (right(t), vcol)),
        pl.BlockSpec((ATT_TQ, aw), lambda t: (t, B_SG_ATT // aw)),
    ]
    return pl.pallas_call(
        functools.partial(_attn_kernel, n_steps),
        out_shape=jax.ShapeDtypeStruct((s, aw), BF16),
        grid=(n_steps,),
        in_specs=in_specs,
        out_specs=pl.BlockSpec((ATT_TQ, aw), lambda t: (t, 0)),
        compiler_params=pltpu.CompilerParams(
            dimension_semantics=("parallel",),
            vmem_limit_bytes=VMEM_LIMIT),
        name="attn",
    )(sink, zb, zb, zb, zb, zb, zb, zb, zb)


MERGE_TM = 256
MERGE_TN = 512


def _merge_kernel(h_ref, pf_ref, pb_ref, sg_ref, yb_ref, ga_ref, gb_ref, cf_ref, cbk_ref,
                  wa_ref, wb_ref, o_ref):
    i = pl.program_id(0)

    def branch_b(cols):
        return jnp.dot(yb_ref[...], wb_ref[:, cols], preferred_element_type=F32)

    first = slice(0, MERGE_TN)
    pb = branch_b(first)
    parts = []
    for q in range(MERGE_TM // LRU_SUB):
        g = i * (MERGE_TM // LRU_SUB) + q
        rows = slice(q * LRU_SUB, (q + 1) * LRU_SUB)
        h = (h_ref[rows, :]
             + pf_ref[rows, :].astype(F32) * cf_ref[pl.ds(g, 1), :]
             + pb_ref[rows, :].astype(F32) * cbk_ref[pl.ds(g, 1), :])
        parts.append((h * sg_ref[rows, :].astype(F32)).astype(BF16))
    y_a = jnp.concatenate(parts, axis=0)
    for c0 in range(0, D_MODEL, MERGE_TN):
        cols = slice(c0, c0 + MERGE_TN)
        if c0 > 0:
            pb = branch_b(cols)
        pa = jnp.dot(y_a, wa_ref[:, cols], preferred_element_type=F32)
        o_ref[:, cols] = (ga_ref[:, cols].astype(F32) * pa
                          + gb_ref[:, cols].astype(F32) * pb).astype(BF16)


def _merge(h_loc, p_f, p_b, y_b, zb, c_f, c_b, wa, wb):
    s = h_loc.shape[0]
    row = lambda i: (i, 0)
    const = lambda i: (0, 0)
    resident = functools.partial(pl.BlockSpec, pipeline_mode=pl.Buffered(1))
    tile = (MERGE_TM, D_MODEL)
    return pl.pallas_call(
        _merge_kernel,
        out_shape=jax.ShapeDtypeStruct((s, D_MODEL), BF16),
        grid=(s // MERGE_TM,),
        in_specs=[
            pl.BlockSpec(tile, row),
            pl.BlockSpec(tile, row),
            pl.BlockSpec(tile, row),
            pl.BlockSpec(tile, lambda i: (i, B_SG_LRU // D_MODEL)),
            pl.BlockSpec(tile, row),
            pl.BlockSpec(tile, lambda i: (i, B_GM_LRU // D_MODEL)),
            pl.BlockSpec(tile, lambda i: (i, B_GM_ATT // D_MODEL)),
            resident(c_f.shape, const),
            resident(c_b.shape, const),
            resident((D_MODEL, D_MODEL), const),
            resident((D_MODEL, D_MODEL), const),
        ],
        out_specs=pl.BlockSpec(tile, row),
        compiler_params=pltpu.CompilerParams(
            dimension_semantics=("parallel",),
            vmem_limit_bytes=VMEM_LIMIT),
        name="merge",
    )(h_loc, p_f, p_b, zb, y_b, zb, zb, c_f, c_b, wa, wb)


OUT_TM = 512


def _out_kernel(m_ref, x_ref, wo_ref, nw_ref, o_ref):
    y = jnp.dot(m_ref[...], wo_ref[...], preferred_element_type=F32)
    ms = jnp.mean(y * y, axis=-1, keepdims=True)
    o_ref[...] = x_ref[...] + y * lax.rsqrt(ms + NORM_EPS) * nw_ref[...]


def _out_proj(merged, x2, wo, norm_w):
    s = x2.shape[0]
    row = lambda i: (i, 0)
    const = lambda i: (0, 0)
    return pl.pallas_call(
        _out_kernel,
        out_shape=jax.ShapeDtypeStruct((s, D_MODEL), F32),
        grid=(s // OUT_TM,),
        in_specs=[
            pl.BlockSpec((OUT_TM, D_MODEL), row),
            pl.BlockSpec((OUT_TM, D_MODEL), row),
            pl.BlockSpec((D_MODEL, D_MODEL), const, pipeline_mode=pl.Buffered(1)),
            pl.BlockSpec((1, D_MODEL), const),
        ],
        out_specs=pl.BlockSpec((OUT_TM, D_MODEL), row),
        compiler_params=pltpu.CompilerParams(
            dimension_semantics=("parallel",),
            vmem_limit_bytes=VMEM_LIMIT),
        name="out_proj",
    )(merged, x2, wo, norm_w)


def _rope_inputs():
    inv_freq = ROPE_THETA ** (-jnp.arange(0, ROT_DIM, 2, dtype=F32) / ROT_DIM)
    freq = jnp.concatenate([inv_freq, inv_freq, jnp.zeros((LANES - ROT_DIM,), F32)])[None, :]
    ang = jnp.arange(INPROJ_TM, dtype=F32)[:, None] * freq
    return freq, jnp.cos(ang), jnp.sin(ang)


def _gate_weights(w_r, w_i):
    w = jnp.concatenate([w_r[0], w_i[0], w_r[1], w_i[1]], axis=-1)
    return (0.5 * w).astype(BF16)


def _lru_params(conv_w, conv_b, b_r, b_i, lam):
    rows = jnp.concatenate([
        conv_w, conv_b[None, :],
        0.5 * jnp.stack([b_r[0], b_i[0], b_r[1], b_i[1]]),
        lam,
        jnp.zeros((P_ROWS - P_LAMBDA - 2, D_MODEL), F32)], axis=0)
    return rows.reshape(P_ROWS, LRU_BLOCKS, LANES).transpose(1, 0, 2)


def kernel(x, norm_pre_w, w_in, conv_w, conv_b, lru_w_r, lru_b_r, lru_w_i, lru_b_i,
           lru_lambda, attn_sink, w_proj_a, w_proj_b, w_out, norm_post_w):
    bsz, s, d = x.shape
    depth = w_in.shape[0]
    rope_inputs = _rope_inputs()
    outs = []
    for bi in range(bsz):
        xb = x[bi]
        for l in range(depth):
            zb, h_loc, p_f, p_b, ef, pef, eb, peb = _inproj(
                xb, norm_pre_w[l][None, :], w_in[l].astype(BF16), rope_inputs,
                _lru_params(conv_w[l], conv_b[l], lru_b_r[l], lru_b_i[l], lru_lambda[l]),
                _gate_weights(lru_w_r[l], lru_w_i[l]))
            c_f, c_b = _lru_carry(ef, pef, eb, peb)
            y_b = _attention(zb, attn_sink[l])
            merged = _merge(h_loc, p_f, p_b, y_b, zb, c_f, c_b,
                            w_proj_a[l].astype(BF16), w_proj_b[l].astype(BF16))
            xb = _out_proj(merged, xb, w_out[l].astype(BF16), norm_post_w[l][None, :])
        outs.append(xb)
    return jnp.stack(outs, axis=0)
```

```python
import functools
import math

import jax
import jax.numpy as jnp
from jax import lax
from jax.experimental import pallas as pl
from jax.experimental.pallas import tpu as pltpu

F32 = jnp.float32
BF16 = jnp.bfloat16

D_MODEL = 2048
LRU_BLOCK_DIM = 128
LRU_BLOCKS = D_MODEL // LRU_BLOCK_DIM
LRU_C = 8.0
N_HEADS = 16
N_KV_HEADS = 4
GROUP = N_HEADS // N_KV_HEADS
HEAD_DIM = 128
KV_WIDTH = N_KV_HEADS * HEAD_DIM
BLOCK = 128
WINDOW = 128
ROPE_THETA = 500000.0
ROT_DIM = HEAD_DIM // 4
NORM_EPS = 1e-6
MASK_VALUE = -1e30
LOG2E = math.log2(math.e)

B_SG_LRU, B_SG_ATT, B_GM_LRU, B_GM_ATT, B_Q, B_K, B_V = (
    0, 2048, 4096, 6144, 8192, 10240, 10752)
B_WIDTH = 11264

SUBLANES = 8
BF16_SUBLANES = 16
LANES = 128
VMEM_LIMIT = 56 * 1024 * 1024


def _sigmoid(x):
    return 0.5 * jnp.tanh(0.5 * x) + 0.5


def _silu(x):
    hx = 0.5 * x
    return hx * jnp.tanh(hx) + hx


def _sqrt_nonneg(x):
    return jnp.where(x > 0.0, x * lax.rsqrt(x), 0.0)


INPROJ_TM = 1024
INPROJ_TN = 1024
ROPE_ROWS = 256
NORM_ROWS = 64
XN_ROWS = INPROJ_TM + BF16_SUBLANES
J_U, J_SILU, J_SIG, J_Q, J_KV, J_END = 0, 2, 6, 10, 12, 13

LRU_SUB = 128
LRU_CW = 256
LRU_NK = LRU_CW // LRU_BLOCK_DIM
LRU_STEPS = D_MODEL // LRU_CW
SLABS_PER_J = INPROJ_TN // LANES
CONV_LEFT = 2
U_ROWS = (LRU_SUB + 3) * SUBLANES
CONV_TILE = 128
COEFF_TILE = 64
GATE_COLS = 4 * LRU_BLOCK_DIM
P_CONV_W, P_CONV_B, P_BIAS, P_LAMBDA, P_ROWS = 0, 4, 5, 9, 16


def _weight_block(j):
    return jnp.where(j < 4, j, jnp.where(j < 10, j + 3, j - 6))


def _rope_tables(freq_ref, rc_ref, rs_ref, chunk, scale):
    lane = lax.broadcasted_iota(jnp.int32, (1, LANES), 1)
    base = (chunk * INPROJ_TM).astype(F32) * freq_ref[...]
    cb, sb = jnp.cos(base) * scale, jnp.sin(base) * scale
    sign = jnp.where(lane < ROT_DIM // 2, -1.0, jnp.where(lane < ROT_DIM, 1.0, 0.0))
    rc, rs = rc_ref[...], rs_ref[...]
    cos_t = cb * rc - sb * rs
    sin_t = (sign * sb) * rc + (sign * cb) * rs
    return cos_t, sin_t


def _rope(t, cos_t, sin_t):
    half = ROT_DIM // 2
    lane = lax.broadcasted_iota(jnp.int32, t.shape, 1)
    partner = jnp.where(lane < half,
                        pltpu.roll(t, HEAD_DIM - half, axis=1),
                        pltpu.roll(t, half, axis=1))
    return t * cos_t + partner * sin_t


def _lru_conv(kp, lp_ref, u_s, uc_s, ucb_s):
    for k in range(LRU_NK):
        kk = kp * LRU_NK + k
        par = lp_ref.at[kk]
        u_k = u_s.at[kk]
        for t0 in range(0, INPROJ_TM, CONV_TILE):
            y = par[P_CONV_B:P_CONV_B + 1, :]
            for tap in range(4):
                lo = t0 + tap * SUBLANES
                y = y + u_k[lo:lo + CONV_TILE, :] * par[P_CONV_W + tap:P_CONV_W + tap + 1, :]
            uc_s[k, t0:t0 + CONV_TILE, :] = y
            ucb_s[k, t0:t0 + CONV_TILE, :] = y.astype(BF16)


def _lru_gates(kp, wg_ref, ucb_s, z_s):
    for k in range(LRU_NK):
        z = jnp.dot(ucb_s[k], wg_ref[kp * LRU_NK + k], preferred_element_type=F32)
        for c, zc_s in enumerate(z_s):
            zc_s[k] = z[:, c * LANES:(c + 1) * LANES]


def _zero_after(x):
    bits = lax.shift_right_logical(pltpu.bitcast(x, jnp.uint32), jnp.uint32(16))
    return pltpu.bitcast(lax.shift_right_logical(bits, jnp.uint32(16)), F32)


def _lru_coeffs(kp, lp_ref, uc_s, ab_s, pace):
    tiles_per_col = (INPROJ_TM // 2) // COEFF_TILE
    for k in range(LRU_NK):
        par = lp_ref.at[kp * LRU_NK + k]
        for d in range(2):
            a_s, b_s = ab_s[2 * d], ab_s[2 * d + 1]
            pace_half = pace[k]
            nl = -par[P_LAMBDA + d:P_LAMBDA + d + 1, :]
            softplus = jnp.maximum(nl, 0.0) + jnp.log1p(jnp.exp(-jnp.abs(nl)))
            dq = -0.25 * LRU_C * softplus
            br_h = par[P_BIAS + 2 * d:P_BIAS + 2 * d + 1, :]
            bi_h = par[P_BIAS + 2 * d + 1:P_BIAS + 2 * d + 2, :]
            for t0 in range(0, INPROJ_TM, COEFF_TILE):
                rows = slice(t0, t0 + COEFF_TILE)
                t = t0 // COEFF_TILE
                p_rows = slice((t % tiles_per_col) * COEFF_TILE, (t % tiles_per_col + 1) * COEFF_TILE)
                p_col = (d * 2 + t // tiles_per_col) * LANES
                zero = _zero_after(pace_half[p_rows, p_col:p_col + LANES])
                tr = jnp.tanh(a_s[k, rows, :] + br_h + zero)
                th = jnp.tanh(tr * dq + dq)
                inv = 1.0 / (1.0 - th)
                root = _sqrt_nonneg(-th)
                ti = jnp.tanh(b_s[k, rows, :] + bi_h)
                a_s[k, rows, :] = (1.0 + th) * inv
                b_s[k, rows, :] = (root * inv) * ((ti + 1.0) * uc_s[k, rows, :])


def _lru_scan(ab_s, h_ref, pf_ref, pb_ref, ef_ref, pef_ref, eb_ref, peb_ref):
    zif_s, zrf_s, zib_s, zrb_s = ab_s
    half = LRU_SUB // 2

    def step(k, d, r, h, p):
        rows = slice(r * SUBLANES, (r + 1) * SUBLANES)
        zr_s, zi_s = (zrf_s, zif_s) if d == 0 else (zrb_s, zib_s)
        a = zi_s[k, rows, :]
        h = a * h + zr_s[k, rows, :]
        p = a * p
        crossed = (r >= half) if d == 0 else (r < half)
        if crossed:
            other = zrb_s if d == 0 else zrf_s
            zrf_s[k, rows, :] = h + other[k, rows, :]
        else:
            zr_s[k, rows, :] = h
        zi_s[k, rows, :] = p
        return h, p

    zero = jnp.zeros((SUBLANES, LANES), F32)
    one = jnp.ones((SUBLANES, LANES), F32)
    state = [[zero, one, zero, one] for _ in range(LRU_NK)]
    for it in range(LRU_SUB):
        for k in range(LRU_NK):
            hf, pf, hb, pb = state[k]
            hf, pf = step(k, 0, it, hf, pf)
            hb, pb = step(k, 1, LRU_SUB - 1 - it, hb, pb)
            state[k] = [hf, pf, hb, pb]

    for k in range(LRU_NK):
        cols = slice(k * LANES, (k + 1) * LANES)
        hf, pf, hb, pb = state[k]
        ef_ref[:, cols] = hf
        pef_ref[:, cols] = pf
        eb_ref[:, cols] = hb
        peb_ref[:, cols] = pb
        for s in range(SUBLANES):
            rows = slice(s * LRU_SUB, (s + 1) * LRU_SUB)
            sel = pl.ds(s, LRU_SUB, stride=SUBLANES)
            h_ref[rows, cols] = zrf_s[k, sel, :]
            pf_ref[rows, cols] = zif_s[k, sel, :].astype(BF16)
            pb_ref[rows, cols] = zib_s[k, sel, :].astype(BF16)


def _inproj_kernel(n_chunks, x_ref, xh_ref, nw_ref, w_ref, freq_ref, rc_ref, rs_ref, lp_ref, wg_ref,
                   b_ref, h_ref, pf_ref, pb_ref, ef_ref, pef_ref, eb_ref, peb_ref,
                   x_buf, x_sem, xn_ref, u_s, uprev_s, uc_s, ucb_s, zrf_s, zif_s, zrb_s, zib_s):
    i = pl.program_id(0)
    j = pl.program_id(1)
    z_s = (zrf_s, zif_s, zrb_s, zib_s)
    lru_outs = (h_ref, pf_ref, pb_ref, ef_ref, pef_ref, eb_ref, peb_ref)

    def x_copy(chunk):
        rows = pl.ds(pl.multiple_of(chunk * INPROJ_TM, INPROJ_TM), INPROJ_TM)
        return pltpu.make_async_copy(x_ref.at[rows, :], x_buf, x_sem)

    @pl.when((i == 0) & (j == 0))
    def _():
        x_copy(0).start()

    @pl.when((j == 1) & (i < n_chunks - 1))
    def _():
        x_copy(i + 1).start()

    def project(lo=0, hi=INPROJ_TM):
        return jnp.dot(xn_ref[lo:hi, :], w_ref[...], preferred_element_type=F32)

    def project_with_lru(epilogue):
        kp = j - J_SILU
        _lru_gates(kp, wg_ref, ucb_s, z_s)
        half = INPROJ_TM // 2
        acc = [project(0, half), project(half, INPROJ_TM)]
        b_ref[0:half, :] = epilogue(acc[0]).astype(BF16)
        b_ref[half:, :] = epilogue(acc[1]).astype(BF16)
        _lru_coeffs(kp, lp_ref, uc_s, z_s, acc)
        _lru_scan(z_s, *lru_outs)
        _lru_conv(jnp.minimum(kp + 1, LRU_STEPS - 1), lp_ref, u_s, uc_s, ucb_s)

    def normalise(xv):
        ms = jnp.mean(xv * xv, axis=-1, keepdims=True)
        return (xv * lax.rsqrt(ms + NORM_EPS) * nw_ref[...]).astype(BF16)

    def u_step(with_norm):
        sub = lax.broadcasted_iota(jnp.int32, (SUBLANES, LANES), 0)
        first_slab = CONV_LEFT * SUBLANES
        for lo in range(0, INPROJ_TM, ROPE_ROWS):
            last = lo + ROPE_ROWS == INPROJ_TM
            if with_norm:
                for r0 in range(lo, lo + ROPE_ROWS, NORM_ROWS):
                    xn_ref[r0:r0 + NORM_ROWS, :] = normalise(x_buf[r0:r0 + NORM_ROWS, :])
                if last:
                    xn_ref[INPROJ_TM:, :] = normalise(xh_ref[...])
            acc = project(lo, XN_ROWS if last else lo + ROPE_ROWS)
            for k in range(SLABS_PER_J):
                u_k = u_s.at[j * SLABS_PER_J + k]
                for s in range(lo // LRU_SUB, (lo + ROPE_ROWS) // LRU_SUB):
                    u_k[pl.ds(first_slab + s, LRU_SUB, stride=SUBLANES), :] = (
                        acc[s * LRU_SUB - lo:(s + 1) * LRU_SUB - lo, k * LANES:(k + 1) * LANES])
        tail = INPROJ_TM - lo
        for k in range(SLABS_PER_J):
            kk = j * SLABS_PER_J + k
            cols = slice(k * LANES, (k + 1) * LANES)
            u_k = u_s.at[kk]
            prev = uprev_s[kk]
            prev1 = jnp.where(i > 0, prev[SUBLANES - 1:SUBLANES, :], 0.0)
            prev2 = jnp.where(i > 0, prev[SUBLANES - 2:SUBLANES - 1, :], 0.0)
            nxt = jnp.where(i < n_chunks - 1, acc[tail:tail + 1, cols], 0.0)
            last1 = u_k[pl.ds(first_slab + (LRU_SUB - 1) * SUBLANES, SUBLANES), :]
            last2 = u_k[pl.ds(first_slab + (LRU_SUB - 2) * SUBLANES, SUBLANES), :]
            head = u_k[pl.ds(first_slab, SUBLANES), :]
            u_k[pl.ds(SUBLANES, SUBLANES), :] = jnp.where(sub == 0, prev1, pltpu.roll(last1, 1, axis=0))
            u_k[pl.ds(0, SUBLANES), :] = jnp.where(sub == 0, prev2, pltpu.roll(last2, 1, axis=0))
            u_k[pl.ds(first_slab + LRU_SUB * SUBLANES, SUBLANES), :] = jnp.where(
                sub == SUBLANES - 1, nxt, pltpu.roll(head, SUBLANES - 1, axis=0))
            uprev_s[kk] = acc[tail - SUBLANES:tail, cols]
        if not with_norm:
            _lru_conv(0, lp_ref, u_s, uc_s, ucb_s)

    @pl.when(j == 0)
    def _():
        x_copy(i).wait()
        u_step(with_norm=True)

    @pl.when(j == 1)
    def _():
        u_step(with_norm=False)

    @pl.when((j >= J_SILU) & (j < J_SIG))
    def _():
        project_with_lru(_silu)

    @pl.when((j >= J_SIG) & (j < J_Q))
    def _():
        project_with_lru(_sigmoid)

    @pl.when((j >= J_Q) & (j < J_KV))
    def _():
        cos_t, sin_t = _rope_tables(freq_ref, rc_ref, rs_ref, i, HEAD_DIM ** -0.5 * LOG2E)
        for lo in range(0, INPROJ_TM, ROPE_ROWS):
            rows = slice(lo, lo + ROPE_ROWS)
            acc = project(lo, lo + ROPE_ROWS)
            for hb in range(INPROJ_TN // HEAD_DIM):
                cols = slice(hb * HEAD_DIM, (hb + 1) * HEAD_DIM)
                b_ref[rows, cols] = _rope(acc[:, cols], cos_t[rows], sin_t[rows]).astype(BF16)

    @pl.when(j == J_KV)
    def _():
        cos_t, sin_t = _rope_tables(freq_ref, rc_ref, rs_ref, i, 1.0)
        for lo in range(0, INPROJ_TM, ROPE_ROWS):
            rows = slice(lo, lo + ROPE_ROWS)
            acc = project(lo, lo + ROPE_ROWS)
            for hb in range(N_KV_HEADS):
                cols = slice(hb * HEAD_DIM, (hb + 1) * HEAD_DIM)
                b_ref[rows, cols] = _rope(acc[:, cols], cos_t[rows], sin_t[rows]).astype(BF16)
            b_ref[rows, KV_WIDTH:] = acc[:, KV_WIDTH:].astype(BF16)


def _inproj(x2, norm_w, w_in_bf16, rope_inputs, lru_params, w_gates):
    s = x2.shape[0]
    n_chunks = s // INPROJ_TM
    n_sub = s // LRU_SUB
    halo_per_chunk = INPROJ_TM // BF16_SUBLANES
    n_halo = s // BF16_SUBLANES
    lru_col = lambda i, j: (i, jnp.clip(j - J_SILU, 0, LRU_STEPS - 1))
    const2 = lambda i, j: (0, 0)
    const3 = lambda i, j: (0, 0, 0)
    scan_scratch = pltpu.VMEM((LRU_NK, INPROJ_TM, LANES), F32)
    return pl.pallas_call(
        functools.partial(_inproj_kernel, n_chunks),
        out_shape=(jax.ShapeDtypeStruct((s, B_WIDTH), BF16),
                   jax.ShapeDtypeStruct((s, D_MODEL), F32),
                   jax.ShapeDtypeStruct((s, D_MODEL), BF16),
                   jax.ShapeDtypeStruct((s, D_MODEL), BF16),
                   jax.ShapeDtypeStruct((n_sub, D_MODEL), F32),
                   jax.ShapeDtypeStruct((n_sub, D_MODEL), F32),
                   jax.ShapeDtypeStruct((n_sub, D_MODEL), F32),
                   jax.ShapeDtypeStruct((n_sub, D_MODEL), F32)),
        grid=(n_chunks, J_END),
        in_specs=[
            pl.BlockSpec(memory_space=pl.ANY),
            pl.BlockSpec((BF16_SUBLANES, D_MODEL),
                         lambda i, j: (jnp.minimum((i + 1) * halo_per_chunk, n_halo - 1), 0)),
            pl.BlockSpec((1, D_MODEL), const2),
            pl.BlockSpec((D_MODEL, INPROJ_TN), lambda i, j: (0, _weight_block(j))),
            pl.BlockSpec((1, LANES), const2),
            pl.BlockSpec((INPROJ_TM, LANES), const2, pipeline_mode=pl.Buffered(1)),
            pl.BlockSpec((INPROJ_TM, LANES), const2, pipeline_mode=pl.Buffered(1)),
            pl.BlockSpec((LRU_BLOCKS, P_ROWS, LANES), const3),
            pl.BlockSpec((LRU_BLOCKS, LRU_BLOCK_DIM, GATE_COLS), const3),
        ],
        out_specs=(
            pl.BlockSpec((INPROJ_TM, INPROJ_TN), lambda i, j: (i, jnp.maximum(j - J_SILU, 0))),
            pl.BlockSpec((INPROJ_TM, LRU_CW), lru_col),
            pl.BlockSpec((INPROJ_TM, LRU_CW), lru_col),
            pl.BlockSpec((INPROJ_TM, LRU_CW), lru_col),
            pl.BlockSpec((SUBLANES, LRU_CW), lru_col),
            pl.BlockSpec((SUBLANES, LRU_CW), lru_col),
            pl.BlockSpec((SUBLANES, LRU_CW), lru_col),
            pl.BlockSpec((SUBLANES, LRU_CW), lru_col),
        ),
        scratch_shapes=[
            pltpu.VMEM((INPROJ_TM, D_MODEL), F32),
            pltpu.SemaphoreType.DMA(()),
            pltpu.VMEM((XN_ROWS, D_MODEL), BF16),
            pltpu.VMEM((LRU_BLOCKS, U_ROWS, LANES), F32),
            pltpu.VMEM((LRU_BLOCKS, SUBLANES, LANES), F32),
            scan_scratch,
            pltpu.VMEM((LRU_NK, INPROJ_TM, LANES), BF16),
            scan_scratch, scan_scratch, scan_scratch, scan_scratch,
        ],
        compiler_params=pltpu.CompilerParams(
            dimension_semantics=("arbitrary", "arbitrary"),
            vmem_limit_bytes=VMEM_LIMIT),
        name="inproj",
    )(x2, x2, norm_w, w_in_bf16, *rope_inputs, lru_params, w_gates)


def _lru_carry_kernel(ef_ref, pef_ref, eb_ref, peb_ref, cf_ref, cbk_ref):
    n_sub = ef_ref.shape[0]

    def fwd(g, carry):
        row = pl.ds(g, 1)
        cf_ref[row, :] = carry
        return ef_ref[row, :] + pef_ref[row, :] * carry

    def bwd(i, carry):
        row = pl.ds(n_sub - 1 - i, 1)
        cbk_ref[row, :] = carry
        return eb_ref[row, :] + peb_ref[row, :] * carry

    zero = jnp.zeros((1, ef_ref.shape[1]), F32)
    lax.fori_loop(0, n_sub, fwd, zero)
    lax.fori_loop(0, n_sub, bwd, zero)


def _lru_carry(ef, pef, eb, peb):
    shape = jax.ShapeDtypeStruct(ef.shape, F32)
    return pl.pallas_call(
        _lru_carry_kernel,
        out_shape=(shape, shape),
        name="lru_carry",
    )(ef, pef, eb, peb)


ATT_QB = 4
ATT_TQ = ATT_QB * BLOCK
ATT_AHEAD = 2
assert WINDOW == BLOCK


def _attn_kernel(n_steps, sink_ref, q_ref, kl_ref, kc_ref, kr_ref, vl_ref, vc_ref, vr_ref,
                 sg_ref, o_ref):
    step = pl.program_id(0)
    k_ext = jnp.concatenate([kl_ref[...], kc_ref[...], kr_ref[...]], axis=0)
    v_ext = jnp.concatenate([vl_ref[...], vc_ref[...], vr_ref[...]], axis=0)

    q_idx = lax.broadcasted_iota(jnp.int32, (BLOCK, BLOCK), 0)
    s_idx = lax.broadcasted_iota(jnp.int32, (BLOCK, BLOCK), 1)
    tri_left = jnp.where(s_idx >= q_idx, 0.0, MASK_VALUE)
    tri_right = jnp.where(s_idx <= q_idx, 0.0, MASK_VALUE)

    def unit_scores(kh, qb):
        q = jnp.concatenate(
            [q_ref[qb * BLOCK:(qb + 1) * BLOCK, (kh * GROUP + g) * HEAD_DIM:(kh * GROUP + g + 1) * HEAD_DIM]
             for g in range(GROUP)], axis=0)
        return lax.dot_general(q, k_ext[qb * BLOCK:(qb + 3) * BLOCK, kh * HEAD_DIM:(kh + 1) * HEAD_DIM],
                               (((1,), (1,)), ((), ())),
                               preferred_element_type=F32)

    units = [(kh, qb) for kh in range(N_KV_HEADS) for qb in range(ATT_QB)]
    pending = [unit_scores(*units[u]) for u in range(ATT_AHEAD)]
    for u, (kh, qb) in enumerate(units):
        scores = pending.pop(0)
        if u + ATT_AHEAD < len(units):
            pending.append(unit_scores(*units[u + ATT_AHEAD]))
        kv_cols = slice(kh * HEAD_DIM, (kh + 1) * HEAD_DIM)
        sinks = [sink_ref[kh * GROUP + g] * LOG2E for g in range(GROUP)]
        bias_l, bias_r = tri_left, tri_right
        if qb == 0:
            bias_l = jnp.where(step > 0, tri_left, MASK_VALUE)
        if qb == ATT_QB - 1:
            bias_r = jnp.where(step < n_steps - 1, tri_right, MASK_VALUE)
        q_rows = slice(qb * BLOCK, (qb + 1) * BLOCK)
        k_rows = slice(qb * BLOCK, (qb + 3) * BLOCK)
        probs, denoms = [], []
        for g in range(GROUP):
            head = scores[g * BLOCK:(g + 1) * BLOCK, :]
            s_l = head[:, :BLOCK] + bias_l
            s_c = head[:, BLOCK:2 * BLOCK]
            s_r = head[:, 2 * BLOCK:] + bias_r
            m = jnp.max(jnp.maximum(jnp.maximum(s_l, s_c), s_r), axis=-1, keepdims=True)
            m = jnp.maximum(m, sinks[g])
            e_l, e_c, e_r = jnp.exp2(s_l - m), jnp.exp2(s_c - m), jnp.exp2(s_r - m)
            denoms.append(jnp.sum(e_l + e_c + e_r, axis=-1, keepdims=True) + jnp.exp2(sinks[g] - m))
            probs.append(jnp.concatenate(
                [e_l.astype(BF16), e_c.astype(BF16), e_r.astype(BF16)], axis=1))
        out = jnp.dot(jnp.concatenate(probs, axis=0), v_ext[k_rows, kv_cols],
                      preferred_element_type=F32)
        for g in range(GROUP):
            cols = slice((kh * GROUP + g) * HEAD_DIM, (kh * GROUP + g + 1) * HEAD_DIM)
            gate = sg_ref[q_rows, cols].astype(F32)
            o_ref[q_rows, cols] = (out[g * BLOCK:(g + 1) * BLOCK, :] / denoms[g] * gate
                                   ).astype(o_ref.dtype)


def _attention(zb, sink):
    s = zb.shape[0]
    n_steps = s // ATT_TQ
    n_blocks = s // BLOCK
    aw = N_HEADS * HEAD_DIM

    def left(t):
        return jnp.maximum(t * ATT_QB - 1, 0)

    def right(t):
        return jnp.minimum((t + 1) * ATT_QB, n_blocks - 1)

    kcol = B_K // KV_WIDTH
    vcol = B_V // KV_WIDTH
    in_specs = [
        pl.BlockSpec(memory_space=pltpu.SMEM),
        pl.BlockSpec((ATT_TQ, aw), lambda t: (t, B_Q // aw)),
        pl.BlockSpec((BLOCK, KV_WIDTH), lambda t: (left(t), kcol)),
        pl.BlockSpec((ATT_TQ, KV_WIDTH), lambda t: (t, kcol)),
        pl.BlockSpec((BLOCK, KV_WIDTH), lambda t: (right(t), kcol)),
        pl.BlockSpec((BLOCK, KV_WIDTH), lambda t: (left(t), vcol)),
        pl.BlockSpec((ATT_TQ, KV_WIDTH), lambda t: (t, vcol)),
        pl.BlockSpec((BLOCK, KV_WIDTH), lambda t: (right(t), vcol)),
        pl.BlockSpec((ATT_TQ, aw), lambda t: (t, B_SG_ATT // aw)),
    ]
    return pl.pallas_call(
        functools.partial(_attn_kernel, n_steps),
        out_shape=jax.ShapeDtypeStruct((s, aw), BF16),
        grid=(n_steps,),
        in_specs=in_specs,
        out_specs=pl.BlockSpec((ATT_TQ, aw), lambda t: (t, 0)),
        compiler_params=pltpu.CompilerParams(
            dimension_semantics=("parallel",),
            vmem_limit_bytes=VMEM_LIMIT),
        name="attn",
    )(sink, zb, zb, zb, zb, zb, zb, zb, zb)


MERGE_TM = 256
MERGE_TN = 512


def _merge_kernel(h_ref, pf_ref, pb_ref, sg_ref, yb_ref, ga_ref, gb_ref, cf_ref, cbk_ref,
                  wa_ref, wb_ref, o_ref):
    i = pl.program_id(0)

    def branch_b(cols):
        return jnp.dot(yb_ref[...], wb_ref[:, cols], preferred_element_type=F32)

    first = slice(0, MERGE_TN)
    pb = branch_b(first)
    parts = []
    for q in range(MERGE_TM // LRU_SUB):
        g = i * (MERGE_TM // LRU_SUB) + q
        rows = slice(q * LRU_SUB, (q + 1) * LRU_SUB)
        h = (h_ref[rows, :]
             + pf_ref[rows, :].astype(F32) * cf_ref[pl.ds(g, 1), :]
             + pb_ref[rows, :].astype(F32) * cbk_ref[pl.ds(g, 1), :])
        parts.append((h * sg_ref[rows, :].astype(F32)).astype(BF16))
    y_a = jnp.concatenate(parts, axis=0)
    for c0 in range(0, D_MODEL, MERGE_TN):
        cols = slice(c0, c0 + MERGE_TN)
        if c0 > 0:
            pb = branch_b(cols)
        pa = jnp.dot(y_a, wa_ref[:, cols], preferred_element_type=F32)
        o_ref[:, cols] = (ga_ref[:, cols].astype(F32) * pa
                          + gb_ref[:, cols].astype(F32) * pb).astype(BF16)


def _merge(h_loc, p_f, p_b, y_b, zb, c_f, c_b, wa, wb):
    s = h_loc.shape[0]
    row = lambda i: (i, 0)
    const = lambda i: (0, 0)
    resident = functools.partial(pl.BlockSpec, pipeline_mode=pl.Buffered(1))
    tile = (MERGE_TM, D_MODEL)
    return pl.pallas_call(
        _merge_kernel,
        out_shape=jax.ShapeDtypeStruct((s, D_MODEL), BF16),
        grid=(s // MERGE_TM,),
        in_specs=[
            pl.BlockSpec(tile, row),
            pl.BlockSpec(tile, row),
            pl.BlockSpec(tile, row),
            pl.BlockSpec(tile, lambda i: (i, B_SG_LRU // D_MODEL)),
            pl.BlockSpec(tile, row),
            pl.BlockSpec(tile, lambda i: (i, B_GM_LRU // D_MODEL)),
            pl.BlockSpec(tile, lambda i: (i, B_GM_ATT // D_MODEL)),
            resident(c_f.shape, const),
            resident(c_b.shape, const),
            resident((D_MODEL, D_MODEL), const),
            resident((D_MODEL, D_MODEL), const),
        ],
        out_specs=pl.BlockSpec(tile, row),
        compiler_params=pltpu.CompilerParams(
            dimension_semantics=("parallel",),
            vmem_limit_bytes=VMEM_LIMIT),
        name="merge",
    )(h_loc, p_f, p_b, zb, y_b, zb, zb, c_f, c_b, wa, wb)


OUT_TM = 512


def _out_kernel(m_ref, x_ref, wo_ref, nw_ref, o_ref):
    y = jnp.dot(m_ref[...], wo_ref[...], preferred_element_type=F32)
    ms = jnp.mean(y * y, axis=-1, keepdims=True)
    o_ref[...] = x_ref[...] + y * lax.rsqrt(ms + NORM_EPS) * nw_ref[...]


def _out_proj(merged, x2, wo, norm_w):
    s = x2.shape[0]
    row = lambda i: (i, 0)
    const = lambda i: (0, 0)
    return pl.pallas_call(
        _out_kernel,
        out_shape=jax.ShapeDtypeStruct((s, D_MODEL), F32),
        grid=(s // OUT_TM,),
        in_specs=[
            pl.BlockSpec((OUT_TM, D_MODEL), row),
            pl.BlockSpec((OUT_TM, D_MODEL), row),
            pl.BlockSpec((D_MODEL, D_MODEL), const, pipeline_mode=pl.Buffered(1)),
            pl.BlockSpec((1, D_MODEL), const),
        ],
        out_specs=pl.BlockSpec((OUT_TM, D_MODEL), row),
        compiler_params=pltpu.CompilerParams(
            dimension_semantics=("parallel",),
            vmem_limit_bytes=VMEM_LIMIT),
        name="out_proj",
    )(merged, x2, wo, norm_w)


def _rope_inputs():
    inv_freq = ROPE_THETA ** (-jnp.arange(0, ROT_DIM, 2, dtype=F32) / ROT_DIM)
    freq = jnp.concatenate([inv_freq, inv_freq, jnp.zeros((LANES - ROT_DIM,), F32)])[None, :]
    ang = jnp.arange(INPROJ_TM, dtype=F32)[:, None] * freq
    return freq, jnp.cos(ang), jnp.sin(ang)


def _gate_weights(w_r, w_i):
    w = jnp.concatenate([w_r[0], w_i[0], w_r[1], w_i[1]], axis=-1)
    return (0.5 * w).astype(BF16)


def _lru_params(conv_w, conv_b, b_r, b_i, lam):
    rows = jnp.concatenate([
        conv_w, conv_b[None, :],
        0.5 * jnp.stack([b_r[0], b_i[0], b_r[1], b_i[1]]),
        lam,
        jnp.zeros((P_ROWS - P_LAMBDA - 2, D_MODEL), F32)], axis=0)
    return rows.reshape(P_ROWS, LRU_BLOCKS, LANES).transpose(1, 0, 2)


def kernel(x, norm_pre_w, w_in, conv_w, conv_b, lru_w_r, lru_b_r, lru_w_i, lru_b_i,
           lru_lambda, attn_sink, w_proj_a, w_proj_b, w_out, norm_post_w):
    bsz, s, d = x.shape
    depth = w_in.shape[0]
    rope_inputs = _rope_inputs()
    outs = []
    for bi in range(bsz):
        xb = x[bi]
        for l in range(depth):
            zb, h_loc, p_f, p_b, ef, pef, eb, peb = _inproj(
                xb, norm_pre_w[l][None, :], w_in[l].astype(BF16), rope_inputs,
                _lru_params(conv_w[l], conv_b[l], lru_b_r[l], lru_b_i[l], lru_lambda[l]),
                _gate_weights(lru_w_r[l], lru_w_i[l]))
            c_f, c_b = _lru_carry(ef, pef, eb, peb)
            y_b = _attention(zb, attn_sink[l])
            merged = _merge(h_loc, p_f, p_b, y_b, zb, c_f, c_b,
                            w_proj_a[l].astype(BF16), w_proj_b[l].astype(BF16))
            xb = _out_proj(merged, xb, w_out[l].astype(BF16), norm_post_w[l][None, :])
        outs.append(xb)
    return jnp.stack(outs, axis=0)
```
